```python
import math
import jax, jax.numpy as jnp
from jax import lax
import numpy as np

D_MODEL = 1024
BATCH = 16
SEQ = 256
DEPTH = 2
DEC_BATCH = 8
DEC_SEQ = 2048
PAST_LEN = 512

GRID_W = 64
BRANCH_W = D_MODEL
N_BRANCH = 3
HY_ORDER = 2
HY_CONV = 3
HY_EMB = 33
HY_FILTER_HIDDEN = 64
HY_DECAY_TARGET = 1e-2
HY_SHORT_DECAY_PCT = 0.3
HY_LONG_DECAY_PCT = 1.5
S5_GROUP_CH = 16
S5_GROUPS = BRANCH_W // S5_GROUP_CH
S5_STATE = 64
GDN_DK = 128
GDN_DV = 128
GDN_HEADS = BRANCH_W // GDN_DV
GDN_CONV = 3
GDN_CHUNK = 64
NORM_EPS = 1e-6
IN_SIZES = ((HY_ORDER + 1) * BRANCH_W, BRANCH_W,
            BRANCH_W, BRANCH_W,
            3 * BRANCH_W, 2 * GDN_HEADS, 2 * GDN_HEADS, BRANCH_W,
            N_BRANCH * D_MODEL)
IN_COLS = sum(IN_SIZES)

kernel_name = 'hybrid_hyena_s5_gdn_diffusion_step'


def rmsnorm(x, g):
    xf = x.astype(jnp.float32)
    y = xf * lax.rsqrt(jnp.mean(xf * xf, axis=-1, keepdims=True) + NORM_EPS)
    return y.astype(x.dtype) * g


def l2norm(x):
    return x * lax.rsqrt(jnp.sum(x * x, axis=-1, keepdims=True) + NORM_EPS)


def dwconv_centred(x, w):
    k = w.shape[0]
    return lax.conv_general_dilated(x, w[:, None, :].astype(x.dtype), window_strides=(1,),
                                    padding=[(k // 2, k // 2)],
                                    dimension_numbers=('NWC', 'WIO', 'NWC'),
                                    feature_group_count=x.shape[-1])


def grid_pos_embed(n_tokens, dim):
    rows = n_tokens // GRID_W
    t = jnp.arange(rows * GRID_W)
    r = (t // GRID_W).astype(jnp.float32)
    col = (t % GRID_W).astype(jnp.float32)
    quarter = dim // 4
    omega = 1.0 / (10000.0 ** (jnp.arange(quarter, dtype=jnp.float32) / quarter))
    er = r[:, None] * omega
    ec = col[:, None] * omega
    return jnp.concatenate([jnp.sin(er), jnp.cos(er), jnp.sin(ec), jnp.cos(ec)], axis=-1)


def hyena_filter_spectra(L, w1, b1, w2, b2, w3, freq):
    f32 = jnp.float32
    t = jnp.linspace(0.0, 1.0, L, dtype=f32)[:, None]
    bands = (HY_EMB - 1) // 2
    fb = jnp.linspace(1e-4, bands - 1, bands, dtype=f32)[None, :]
    wpos = 2.0 * math.pi * jnp.arange(L, dtype=f32)[:, None] / L
    z = jnp.concatenate([t, jnp.cos(fb * wpos), -jnp.sin(fb * wpos)], axis=-1)
    freq = freq.astype(f32)
    hdn = jnp.sin(freq * (z @ w1.astype(f32) + b1.astype(f32)))
    hdn = jnp.sin(freq * (hdn @ w2.astype(f32) + b2.astype(f32)))
    filt = (hdn @ w3.astype(f32)).reshape(L, HY_ORDER, 2, BRANCH_W)
    deltas = jnp.abs(jnp.linspace(math.log(HY_DECAY_TARGET) / HY_LONG_DECAY_PCT,
                                  math.log(HY_DECAY_TARGET) / HY_SHORT_DECAY_PCT, BRANCH_W, dtype=f32))
    filt = filt * jnp.exp(-t * deltas)[:, None, None, :]
    fwd, bwd = filt[:, :, 0], filt[:, :, 1]
    k = jnp.concatenate([fwd, jnp.zeros((1, HY_ORDER, BRANCH_W), f32), jnp.flip(bwd[1:], axis=0)], axis=0)
    k = k / jnp.sum(jnp.abs(k), axis=0, keepdims=True)
    return jnp.fft.rfft(k, axis=0)


def hyena_mixer(u, kf, bias):
    L = u.shape[1]
    parts = jnp.split(u, HY_ORDER + 1, axis=-1)
    z = parts[0]
    for o in range(HY_ORDER):
        conv = jnp.fft.irfft(jnp.fft.rfft(z, n=2 * L, axis=1) * kf[None, :, o], n=2 * L, axis=1)[:, :L]
        z = parts[o + 1] * (conv + bias[o] * z)
    return z


def _lin_combine(e1, e2):
    a1, b1 = e1
    a2, b2 = e2
    return a1 * a2, a2 * b1 + b2


def s5_mixer(u, lam_re, lam_im, log_step, b_re, b_im, c_re, c_im, d_skip, s0):
    f32 = jnp.float32
    bsz, L, _ = u.shape
    ug = u.reshape(bsz, L, S5_GROUPS, S5_GROUP_CH)
    y = d_skip.astype(f32).reshape(S5_GROUPS, S5_GROUP_CH) * ug
    s0 = s0.astype(f32)
    finals = []
    for d in range(2):
        lam = lax.complex(lam_re[d].astype(f32), lam_im[d].astype(f32))
        step = jnp.exp(log_step[d].astype(f32))[:, None]
        lam_bar = jnp.exp(lam * step)
        b_bar = ((lam_bar - 1.0) / lam)[..., None] * lax.complex(b_re[d].astype(f32), b_im[d].astype(f32))
        init = lax.complex(s0[:, d, :, :, 0], s0[:, d, :, :, 1])
        seq = ug if d == 0 else jnp.flip(ug, axis=1)
        bu = jnp.einsum('gph,blgh->blgp', b_bar, seq.astype(jnp.complex64))
        bu = bu.at[:, 0].add(lam_bar * init)
        a = jnp.broadcast_to(lam_bar, bu.shape)
        _, xs = lax.associative_scan(_lin_combine, (a, bu), axis=1)
        finals.append(xs[:, -1])
        yd = (jnp.einsum('ghp,blgp->blgh', c_re[d].astype(f32), xs.real)
              - jnp.einsum('ghp,blgp->blgh', c_im[d].astype(f32), xs.imag))
        y = y + (yd if d == 0 else jnp.flip(yd, axis=1))
    fin = jnp.stack(finals, axis=1)
    return y.reshape(bsz, L, S5_GROUPS * S5_GROUP_CH), jnp.stack([fin.real, fin.imag], axis=-1)


def chunk_gated_delta(q, k, v, g, beta, s0):
    bsz, L, h, _ = k.shape
    dv = v.shape[-1]
    C = GDN_CHUNK
    n = L // C

    def blk(t):
        return jnp.moveaxis(t.reshape((bsz, n, C, h) + t.shape[3:]), 3, 1)

    q, k, v, g, beta = blk(q), blk(k), blk(v), blk(g), blk(beta)
    gc = jnp.cumsum(g, axis=-1)
    lower = jnp.tril(jnp.ones((C, C), dtype=bool))
    strict = jnp.tril(jnp.ones((C, C), dtype=bool), -1)
    diff = gc[..., :, None] - gc[..., None, :]
    decay = jnp.where(lower, jnp.exp(jnp.where(lower, diff, 0.0)), 0.0)
    kb = k * beta[..., None]
    a_mat = jnp.where(strict, jnp.einsum('bhncd,bhnsd->bhncs', kb, k) * decay, 0.0)
    eye = jnp.eye(C, dtype=jnp.float32)
    t_mat = lax.linalg.triangular_solve(a_mat + eye, jnp.broadcast_to(eye, a_mat.shape),
                                        left_side=True, lower=True, unit_diagonal=True)
    u = jnp.einsum('bhncs,bhnsv->bhncv', t_mat, v * beta[..., None])
    w = jnp.einsum('bhncs,bhnsd->bhncd', t_mat, kb * jnp.exp(gc)[..., None])
    qk = jnp.where(lower, jnp.einsum('bhncd,bhnsd->bhncs', q, k) * decay, 0.0)

    def step(S, inp):
        q_i, k_i, u_i, w_i, g_i, qk_i = inp
        v_new = u_i - jnp.einsum('bhcd,bhdv->bhcv', w_i, S)
        o = (jnp.einsum('bhcd,bhdv->bhcv', q_i * jnp.exp(g_i)[..., None], S)
             + jnp.einsum('bhcs,bhsv->bhcv', qk_i, v_new))
        g_last = g_i[..., -1:]
        S = S * jnp.exp(g_last)[..., None] + jnp.einsum(
            'bhcd,bhcv->bhdv', k_i * jnp.exp(g_last - g_i)[..., None], v_new)
        return S, o

    xs = tuple(jnp.moveaxis(t, 2, 0) for t in (q, k, u, w, gc, qk))
    s_fin, o = lax.scan(step, s0, xs)
    o = jnp.moveaxis(jnp.moveaxis(o, 0, 2), 1, 3).reshape(bsz, L, h, dv)
    return o, s_fin


def gdn_mixer(qkv, beta_in, alpha_in, a_log, dt_bias, norm_g, s0):
    f32 = jnp.float32
    bsz, L, _ = qkv.shape
    q, k, v = jnp.split(qkv, 3, axis=-1)
    q = l2norm(q.reshape(bsz, L, GDN_HEADS, GDN_DK)) * (GDN_DK ** -0.5)
    k = l2norm(k.reshape(bsz, L, GDN_HEADS, GDN_DK))
    v = v.reshape(bsz, L, GDN_HEADS, GDN_DV)
    beta = jax.nn.sigmoid(beta_in.reshape(bsz, L, 2, GDN_HEADS))
    g = -jnp.exp(a_log.astype(f32)) * jax.nn.softplus(alpha_in.reshape(bsz, L, 2, GDN_HEADS) + dt_bias.astype(f32))
    s0 = s0.astype(f32)
    o_f, s_f = chunk_gated_delta(q, k, v, g[:, :, 0], beta[:, :, 0], s0[:, 0])
    rev = lambda t: jnp.flip(t, axis=1)
    o_b, s_b = chunk_gated_delta(rev(q), rev(k), rev(v), rev(g[:, :, 1]), rev(beta[:, :, 1]), s0[:, 1])
    o = rmsnorm(o_f + rev(o_b), norm_g.astype(f32))
    return o.reshape(bsz, L, GDN_HEADS * GDN_DV), jnp.stack([s_f, s_b], axis=1)


def trunk_layer(x, mod, s0_gdn, s0_s5, lp):
    f32 = jnp.float32
    bsz, L, _ = x.shape
    shift, scale, gate = jnp.split(mod, 3, axis=-1)
    h = rmsnorm(x, lp['norm_g']) * (1.0 + scale) + shift
    z = h @ lp['w_in']
    cuts = np.cumsum(IN_SIZES)[:-1].tolist()
    hy_in, hy_gate, s5_in, s5_gate, gdn_in, gdn_beta, gdn_alpha, gdn_gate, merge_in = jnp.split(z, cuts, axis=-1)
    hy_u = dwconv_centred(hy_in, lp['hy_conv_w']) + lp['hy_conv_b']
    kf = hyena_filter_spectra(L, lp['hy_f_w1'], lp['hy_f_b1'], lp['hy_f_w2'], lp['hy_f_b2'],
                              lp['hy_f_w3'], lp['hy_f_freq'])
    y_hy = hyena_mixer(hy_u.astype(f32), kf, lp['hy_bias'].astype(f32)) * jax.nn.silu(hy_gate.astype(f32))
    s5_raw, s5_fin = s5_mixer(s5_in.astype(f32), lp['s5_lambda_re'], lp['s5_lambda_im'], lp['s5_log_step'],
                              lp['s5_B_re'], lp['s5_B_im'], lp['s5_C_re'], lp['s5_C_im'], lp['s5_D'], s0_s5)
    glu = jax.nn.gelu(s5_raw) @ lp['s5_glu_w'].astype(f32) + lp['s5_glu_b'].astype(f32)
    glu_a, glu_g = jnp.split(glu, 2, axis=-1)
    y_s5 = glu_a * jax.nn.sigmoid(glu_g) * jax.nn.silu(s5_gate.astype(f32))
    qkv = jax.nn.silu(dwconv_centred(gdn_in, lp['gdn_conv_w']).astype(f32))
    o_gdn, gdn_fin = gdn_mixer(qkv, gdn_beta.astype(f32), gdn_alpha.astype(f32), lp['gdn_A_log'],
                               lp['gdn_dt_bias'], lp['gdn_norm_g'], s0_gdn)
    y_gdn = o_gdn * jax.nn.silu(gdn_gate.astype(f32))
    ys = jnp.stack([y_hy, y_s5, y_gdn], axis=2).astype(x.dtype)
    proj = jnp.einsum('blnw,nwd->blnd', ys, lp['w_branch'])
    gates = jax.nn.sigmoid(merge_in.reshape(bsz, L, N_BRANCH, D_MODEL))
    merged = jnp.sum(gates * proj, axis=2)
    x = x + gate * (merged @ lp['w_out'])
    return x, gdn_fin, s5_fin


def setup_inputs(seed: int = 0) -> dict:
    key = jax.random.key(seed)
    ks = iter(jax.random.split(key, 48))
    f32 = jnp.float32
    W, H, G, P = BRANCH_W, GDN_HEADS, S5_GROUPS, S5_STATE

    def nrm(shape, s):
        return s * jax.random.normal(next(ks), shape, f32)

    n_idx = jnp.arange(P, dtype=f32)
    gdn_dt = jnp.exp(jax.random.uniform(next(ks), (DEPTH, 2, H), f32, math.log(1e-3), math.log(1e-1)))
    return {
        'x_prompt': nrm((BATCH, SEQ, D_MODEL), 1.0),
        'x_sample': nrm((DEC_BATCH, DEC_SEQ, D_MODEL), 1.0),
        'c': nrm((DEC_BATCH, D_MODEL), 1.0),
        'c_ctx': nrm((D_MODEL,), 1.0),
        'state_gdn': nrm((DEC_BATCH, DEPTH, 2, H, GDN_DK, GDN_DV), 0.1),
        'state_s5': nrm((DEC_BATCH, DEPTH, 2, G, P, 2), 0.05),
        'w_mod': nrm((DEPTH, D_MODEL, 3 * D_MODEL), 0.5 * D_MODEL ** -0.5),
        'b_mod': nrm((DEPTH, 3 * D_MODEL), 0.01),
        'norm_g': 1.0 + nrm((DEPTH, D_MODEL), 0.01),
        'w_in': nrm((DEPTH, D_MODEL, IN_COLS), D_MODEL ** -0.5),
        'hy_conv_w': nrm((DEPTH, HY_CONV, (HY_ORDER + 1) * W), HY_CONV ** -0.5),
        'hy_conv_b': nrm((DEPTH, (HY_ORDER + 1) * W), 0.01),
        'hy_f_w1': nrm((DEPTH, HY_EMB, HY_FILTER_HIDDEN), HY_EMB ** -0.5),
        'hy_f_b1': nrm((DEPTH, HY_FILTER_HIDDEN), 0.1),
        'hy_f_w2': nrm((DEPTH, HY_FILTER_HIDDEN, HY_FILTER_HIDDEN), HY_FILTER_HIDDEN ** -0.5),
        'hy_f_b2': nrm((DEPTH, HY_FILTER_HIDDEN), 0.1),
        'hy_f_w3': nrm((DEPTH, HY_FILTER_HIDDEN, HY_ORDER * 2 * W), HY_FILTER_HIDDEN ** -0.5),
        'hy_f_freq': 1.0 + nrm((DEPTH, HY_FILTER_HIDDEN), 0.1),
        'hy_bias': nrm((DEPTH, HY_ORDER, W), 0.5),
        's5_lambda_re': -0.5 + nrm((DEPTH, 2, G, P), 0.01),
        's5_lambda_im': math.pi * n_idx + nrm((DEPTH, 2, G, P), 0.01),
        's5_log_step': jax.random.uniform(next(ks), (DEPTH, 2, G), f32, math.log(1e-3), math.log(1e-1)),
        's5_B_re': nrm((DEPTH, 2, G, P, S5_GROUP_CH), (2 * S5_GROUP_CH) ** -0.5),
        's5_B_im': nrm((DEPTH, 2, G, P, S5_GROUP_CH), (2 * S5_GROUP_CH) ** -0.5),
        's5_C_re': nrm((DEPTH, 2, G, S5_GROUP_CH, P), (2 * P) ** -0.5),
        's5_C_im': nrm((DEPTH, 2, G, S5_GROUP_CH, P), (2 * P) ** -0.5),
        's5_D': nrm((DEPTH, W), 1.0),
        's5_glu_w': nrm((DEPTH, W, 2 * W), W ** -0.5),
        's5_glu_b': nrm((DEPTH, 2 * W), 0.01),
        'gdn_conv_w': nrm((DEPTH, GDN_CONV, 3 * W), GDN_CONV ** -0.5),
        'gdn_A_log': jnp.log(jax.random.uniform(next(ks), (DEPTH, 2, H), f32, 1.0, 16.0)),
        'gdn_dt_bias': gdn_dt + jnp.log(-jnp.expm1(-gdn_dt)),
        'gdn_norm_g': 1.0 + nrm((DEPTH, GDN_DV), 0.01),
        'w_branch': nrm((DEPTH, N_BRANCH, W, D_MODEL), W ** -0.5),
        'w_out': nrm((DEPTH, D_MODEL, D_MODEL), D_MODEL ** -0.5),
        'final_norm_g': 1.0 + nrm((D_MODEL,), 0.01),
    }


def reference(x_prompt, x_sample, c, c_ctx, state_gdn, state_s5, w_mod, b_mod, norm_g, w_in,
              hy_conv_w, hy_conv_b, hy_f_w1, hy_f_b1, hy_f_w2, hy_f_b2, hy_f_w3, hy_f_freq, hy_bias,
              s5_lambda_re, s5_lambda_im, s5_log_step, s5_B_re, s5_B_im, s5_C_re, s5_C_im, s5_D,
              s5_glu_w, s5_glu_b, gdn_conv_w, gdn_A_log, gdn_dt_bias, gdn_norm_g, w_branch, w_out,
              final_norm_g):
    bp = x_prompt.shape[0]
    xc = x_prompt
    xl = x_sample + grid_pos_embed(x_sample.shape[1], D_MODEL).astype(x_sample.dtype)
    zero_gdn = jnp.zeros((bp, 2, GDN_HEADS, GDN_DK, GDN_DV), jnp.float32)
    zero_s5 = jnp.zeros((bp, 2, S5_GROUPS, S5_STATE, 2), jnp.float32)
    gdn_states = []
    s5_states = []
    for l in range(DEPTH):
        lp = dict(norm_g=norm_g[l], w_in=w_in[l], hy_conv_w=hy_conv_w[l], hy_conv_b=hy_conv_b[l],
                  hy_f_w1=hy_f_w1[l], hy_f_b1=hy_f_b1[l], hy_f_w2=hy_f_w2[l], hy_f_b2=hy_f_b2[l],
                  hy_f_w3=hy_f_w3[l], hy_f_freq=hy_f_freq[l], hy_bias=hy_bias[l],
                  s5_lambda_re=s5_lambda_re[l], s5_lambda_im=s5_lambda_im[l], s5_log_step=s5_log_step[l],
                  s5_B_re=s5_B_re[l], s5_B_im=s5_B_im[l], s5_C_re=s5_C_re[l], s5_C_im=s5_C_im[l],
                  s5_D=s5_D[l], s5_glu_w=s5_glu_w[l], s5_glu_b=s5_glu_b[l], gdn_conv_w=gdn_conv_w[l],
                  gdn_A_log=gdn_A_log[l], gdn_dt_bias=gdn_dt_bias[l], gdn_norm_g=gdn_norm_g[l],
                  w_branch=w_branch[l], w_out=w_out[l])
        mod_ctx = (jax.nn.silu(c_ctx) @ w_mod[l] + b_mod[l])[None, None, :]
        mod_lat = (jax.nn.silu(c) @ w_mod[l] + b_mod[l])[:, None, :]
        xc, fin_gdn, fin_s5 = trunk_layer(xc, mod_ctx, zero_gdn, zero_s5, lp)
        gdn_states.append(fin_gdn)
        s5_states.append(fin_s5)
        xl, _, _ = trunk_layer(xl, mod_lat, state_gdn[:, l], state_s5[:, l], lp)
    y_prompt = rmsnorm(xc, final_norm_g)
    y_sample = rmsnorm(xl, final_norm_g)
    new_state_gdn = jnp.stack(gdn_states, axis=1)
    new_state_s5 = jnp.stack(s5_states, axis=1)
    return (y_prompt, y_sample, new_state_gdn, new_state_s5)
```

```python
import functools
import math

import jax
import jax.numpy as jnp
from jax import lax
from jax.experimental import pallas as pl
from jax.experimental.pallas import tpu as pltpu

F32 = jnp.float32
BF16 = jnp.bfloat16

D_MODEL = 1024
BRANCH_W = 1024
GRID_W = 64
HY_ORDER = 2
HY_EMB = 33
HY_HID = 64
HY_DECAY_TARGET = 1e-2
HY_SHORT_DECAY_PCT = 0.3
HY_LONG_DECAY_PCT = 1.5
S5_GROUP_CH = 16
S5_GROUPS = 64
S5_STATE = 64
GDN_DK = 128
GDN_DV = 128
GDN_HEADS = 8
GDN_CHUNK = 64
NORM_EPS = 1e-6

LANES = 128
Z_HY_IN, Z_HY_GATE, Z_S5_IN, Z_S5_GATE = 0, 3072, 4096, 5120
Z_GDN_IN, Z_GDN_GATE, Z_MERGE, Z_AB = 6144, 9216, 10240, 13312
NZ = 14336
W_IN_AB = 9216
VMEM_LIMIT_V7X = 56 * 1024 * 1024


def _cparams(sem):
    return pltpu.CompilerParams(dimension_semantics=sem, vmem_limit_bytes=VMEM_LIMIT_V7X)


def _dot(a, b):
    return jnp.dot(a, b, preferred_element_type=F32)


def _bdot(a, b):
    return jnp.dot(a.astype(BF16), b.astype(BF16), preferred_element_type=F32)


def _split(a):
    hi = a.astype(BF16)
    lo = (a - hi.astype(F32)).astype(BF16)
    return hi, lo


def _dot3(a, b):
    ah, al = _split(a)
    bh, bl = _split(b)
    return _dot(ah, bh) + _dot(al, bh) + _dot(ah, bl)


def _silu(x):
    return x * jax.nn.sigmoid(x)


def _shift_prev(x):
    r = lax.broadcasted_iota(jnp.int32, x.shape, 0)
    return jnp.where(r == 0, 0.0, pltpu.roll(x, 1, axis=0))


def _shift_next(x):
    n = x.shape[0]
    r = lax.broadcasted_iota(jnp.int32, x.shape, 0)
    return jnp.where(r == n - 1, 0.0, pltpu.roll(x, n - 1, axis=0))


def _dwconv3(x, w):
    return w[0:1] * _shift_prev(x) + w[1:2] * x + w[2:3] * _shift_next(x)


def _mod_kernel(c_ref, w_ref, b_ref, o_ref):
    o_ref[0] = _dot3(_silu(c_ref[...]), w_ref[0]) + b_ref[0]


def _modulation(cc, w_mod, b_mod):
    depth = w_mod.shape[0]
    r = cc.shape[0]
    return pl.pallas_call(
        _mod_kernel,
        grid=(depth, 3),
        in_specs=[pl.BlockSpec((r, D_MODEL), lambda l, n: (0, 0)),
                  pl.BlockSpec((1, D_MODEL, D_MODEL), lambda l, n: (l, 0, n)),
                  pl.BlockSpec((1, 1, D_MODEL), lambda l, n: (l, 0, n))],
        out_specs=pl.BlockSpec((1, r, D_MODEL), lambda l, n: (l, 0, n)),
        out_shape=jax.ShapeDtypeStruct((depth, r, 3 * D_MODEL), F32),
        compiler_params=_cparams(("arbitrary", "arbitrary")),
        name="modulation",
    )(cc, w_mod, b_mod.reshape(depth, 1, 3 * D_MODEL))


def _inproj_kernel(*refs, has_pos):
    if has_pos:
        x_ref, pos_ref, mod_ref, g_ref, w_ref, z_ref, xres_ref, h_scr = refs
    else:
        x_ref, mod_ref, g_ref, w_ref, z_ref, h_scr = refs
    tl, b, d = x_ref.shape

    @pl.when(pl.program_id(1) == 0)
    def _():
        x = x_ref[...]
        if has_pos:
            x = x + pos_ref[...]
            xres_ref[...] = x
        y = x * lax.rsqrt(jnp.mean(x * x, axis=-1, keepdims=True) + NORM_EPS) * g_ref[...]
        h = y * (1.0 + mod_ref[1]) + mod_ref[0]
        h_scr[...] = h.reshape(tl * b, d).astype(BF16)

    z = _dot(h_scr[...], w_ref[...])
    z_ref[...] = z.reshape(tl, b, z.shape[-1])


def _in_projection(x, pos, mod3, norm_g, w_packed, *, rows_per_tile=1024, tn=2048):
    L, B, _ = x.shape
    tl = min(L, rows_per_tile // B)
    grid = (L // tl, NZ // tn)
    has_pos = pos is not None
    in_specs = [pl.BlockSpec((tl, B, D_MODEL), lambda i, j: (i, 0, 0))]
    args = [x]
    if has_pos:
        in_specs.append(pl.BlockSpec((tl, 1, D_MODEL), lambda i, j: (i, 0, 0)))
        args.append(pos)
    in_specs += [pl.BlockSpec((3, B, D_MODEL), lambda i, j: (0, 0, 0)),
                 pl.BlockSpec((1, 1, D_MODEL), lambda i, j: (0, 0, 0)),
                 pl.BlockSpec((D_MODEL, tn), lambda i, j: (0, j))]
    args += [mod3, norm_g.reshape(1, 1, D_MODEL), w_packed]
    out_specs = [pl.BlockSpec((tl, B, tn), lambda i, j: (i, 0, j))]
    out_shape = [jax.ShapeDtypeStruct((L, B, NZ), F32)]
    if has_pos:
        out_specs.append(pl.BlockSpec((tl, B, D_MODEL), lambda i, j: (i, 0, 0)))
        out_shape.append(jax.ShapeDtypeStruct((L, B, D_MODEL), F32))
    res = pl.pallas_call(
        functools.partial(_inproj_kernel, has_pos=has_pos),
        grid=grid, in_specs=in_specs, out_specs=out_specs, out_shape=out_shape,
        scratch_shapes=[pltpu.VMEM((tl * B, D_MODEL), BF16)],
        compiler_params=_cparams(("parallel", "arbitrary")),
        name="in_projection",
    )(*args)
    return (res[0], res[1]) if has_pos else (res[0], x)


def _dft_tables(L):
    f = lax.broadcasted_iota(jnp.int32, (L, L), 0)
    t = lax.broadcasted_iota(jnp.int32, (L, L), 1)
    ang = ((f * t) % (2 * L)).astype(F32) * (math.pi / L)
    c = jnp.cos(ang)
    s = jnp.sin(ang)
    alt = jnp.where(t % 2 == 0, 1.0, -1.0).astype(F32)
    fwd_c = c
    fwd_s = jnp.where(f == 0, alt, s)
    fi, ti = t, f
    inv_c = jnp.where(fi == 0, 1.0, 2.0) * c / (2.0 * L)
    inv_s = jnp.where(fi == 0, jnp.where(ti % 2 == 0, 1.0, -1.0), 2.0 * s) / (2.0 * L)
    return fwd_c.astype(BF16), fwd_s.astype(BF16), inv_c.astype(BF16), inv_s.astype(BF16)


def _filter_tables(L):
    t = jnp.linspace(0.0, 1.0, L, dtype=F32)[:, None]
    bands = (HY_EMB - 1) // 2
    fb = jnp.linspace(1e-4, bands - 1, bands, dtype=F32)[None, :]
    wpos = 2.0 * math.pi * jnp.arange(L, dtype=F32)[:, None] / L
    zpos = jnp.concatenate([t, jnp.cos(fb * wpos), -jnp.sin(fb * wpos)], axis=-1)
    deltas = jnp.abs(jnp.linspace(math.log(HY_DECAY_TARGET) / HY_LONG_DECAY_PCT,
                                  math.log(HY_DECAY_TARGET) / HY_SHORT_DECAY_PCT, BRANCH_W, dtype=F32))
    decay = jnp.exp(-t * deltas)
    return zpos, decay


def _hy_filter_kernel(zpos_ref, w1_ref, b1_ref, w2_ref, b2_ref, freq_ref, w3_ref, decay_ref,
                      fc_ref, fs_ref, p_ref, q_ref, kl_ref):
    freq = freq_ref[...]
    hdn = jnp.sin(freq * (_dot3(zpos_ref[...], w1_ref[...]) + b1_ref[...]))
    hdn = jnp.sin(freq * (_dot3(hdn, w2_ref[...]) + b2_ref[...]))
    decay = decay_ref[...]
    row = lax.broadcasted_iota(jnp.int32, decay.shape, 0)
    alt = jnp.where(row % 2 == 0, 1.0, -1.0)
    for o in range(HY_ORDER):
        fwd = _dot3(hdn, w3_ref[2 * o]) * decay
        bwd = jnp.where(row == 0, 0.0, _dot3(hdn, w3_ref[2 * o + 1]) * decay)
        norm = jnp.sum(jnp.abs(fwd) + jnp.abs(bwd), axis=0, keepdims=True)
        a = (fwd + bwd) / norm
        s = (fwd - bwd) / norm
        p_ref[o] = _bdot(fc_ref[...], a)
        q_ref[o] = jnp.where(row == 0, 0.0, _bdot(fs_ref[...], s))
        kl_ref[o] = jnp.sum(alt * a, axis=0, keepdims=True)


def _hyena_filter_spectra(L, lp, tables, *, cb=256):
    zpos, decay, fc, fs = tables
    w3 = jnp.transpose(lp['hy_f_w3'].reshape(HY_HID, HY_ORDER * 2, BRANCH_W), (1, 0, 2))
    const = lambda *shape: pl.BlockSpec(shape, lambda j: (0,) * len(shape))
    return pl.pallas_call(
        _hy_filter_kernel,
        grid=(BRANCH_W // cb,),
        in_specs=[const(L, HY_EMB), const(HY_EMB, HY_HID), const(1, HY_HID), const(HY_HID, HY_HID),
                  const(1, HY_HID), const(1, HY_HID),
                  pl.BlockSpec((HY_ORDER * 2, HY_HID, cb), lambda j: (0, 0, j)),
                  pl.BlockSpec((L, cb), lambda j: (0, j)),
                  const(L, L), const(L, L)],
        out_specs=[pl.BlockSpec((HY_ORDER, L, cb), lambda j: (0, 0, j)),
                   pl.BlockSpec((HY_ORDER, L, cb), lambda j: (0, 0, j)),
                   pl.BlockSpec((HY_ORDER, 1, cb), lambda j: (0, 0, j))],
        out_shape=[jax.ShapeDtypeStruct((HY_ORDER, L, BRANCH_W), F32),
                   jax.ShapeDtypeStruct((HY_ORDER, L, BRANCH_W), F32),
                   jax.ShapeDtypeStruct((HY_ORDER, 1, BRANCH_W), F32)],
        compiler_params=_cparams(("arbitrary",)),
        name="hyena_filter",
    )(zpos, lp['hy_f_w1'], lp['hy_f_b1'].reshape(1, HY_HID), lp['hy_f_w2'], lp['hy_f_b2'].reshape(1, HY_HID),
      lp['hy_f_freq'].reshape(1, HY_HID), w3, decay, fc, fs)


def _hy_fwd_kernel(*refs, first):
    if first:
        zin_ref, cw_ref, cbias_ref, p_ref, q_ref, kl_ref, fc_ref, fs_ref, yr_ref, yn_ref = refs
        zin = _dwconv3(zin_ref[...], cw_ref[...]) + cbias_ref[...]
    else:
        zin_ref, p_ref, q_ref, kl_ref, fc_ref, fs_ref, yr_ref, yn_ref = refs
        zin = zin_ref[...]
    zb = zin.astype(BF16)
    zr = _dot(fc_ref[...], zb)
    w = _dot(fs_ref[...], zb)
    p = p_ref[0]
    q = q_ref[0]
    row = lax.broadcasted_iota(jnp.int32, p.shape, 0)
    pd = jnp.where(row == 0, kl_ref[0], p)
    yr_ref[...] = (zr * p - w * q).astype(BF16)
    yn_ref[...] = (zr * q + w * pd).astype(BF16)


def _hy_inv_kernel(*refs, first):
    if first:
        (yr_ref, yn_ref, gc_ref, gs_ref, zin_ref, cwz_ref, cbz_ref, xg_ref, cwx_ref, cbx_ref,
         bias_ref, out_ref) = refs
        zin = _dwconv3(zin_ref[...], cwz_ref[...]) + cbz_ref[...]
    else:
        yr_ref, yn_ref, gc_ref, gs_ref, zin_ref, xg_ref, cwx_ref, cbx_ref, bias_ref, out_ref = refs
        zin = zin_ref[...]
    conv = _dot(gc_ref[...], yr_ref[...]) + _dot(gs_ref[...], yn_ref[...])
    xg = _dwconv3(xg_ref[...], cwx_ref[...]) + cbx_ref[...]
    out_ref[...] = xg * (conv + bias_ref[0] * zin)


def _hyena(z2d, B, lp, spectra, dft, *, cb=256):
    L = z2d.shape[0]
    P, Q, KL = spectra
    fc, fs, gc, gs = dft
    ncb = BRANCH_W // cb
    zb = NZ // cb
    cw = lp['hy_conv_w']
    cbias = lp['hy_conv_b'].reshape(1, -1)
    bias = lp['hy_bias'].reshape(HY_ORDER, 1, BRANCH_W)
    grid = (ncb, B)
    zspec = lambda part: pl.BlockSpec((L, cb), lambda j, b: (0, b * zb + part * ncb + j))
    wspec = lambda part: pl.BlockSpec((3, cb), lambda j, b: (0, part * ncb + j))
    bspec = lambda part: pl.BlockSpec((1, cb), lambda j, b: (0, part * ncb + j))
    aspec = pl.BlockSpec((L, cb), lambda j, b: (0, b * ncb + j))
    dspec = pl.BlockSpec((L, L), lambda j, b: (0, 0))
    ospec = lambda o: pl.BlockSpec((1, L, cb), lambda j, b: (o, 0, j))
    kspec = lambda o: pl.BlockSpec((1, 1, cb), lambda j, b: (o, 0, j))
    cp = _cparams(("parallel", "arbitrary"))
    act = lambda dt: jax.ShapeDtypeStruct((L, B * BRANCH_W), dt)

    zcur = None
    for o in range(HY_ORDER):
        first = o == 0
        if first:
            ins, args = [zspec(0), wspec(0), bspec(0)], [z2d, cw, cbias]
        else:
            ins, args = [aspec], [zcur]
        yr, yn = pl.pallas_call(
            functools.partial(_hy_fwd_kernel, first=first), grid=grid,
            in_specs=ins + [ospec(o), ospec(o), kspec(o), dspec, dspec],
            out_specs=[aspec, aspec], out_shape=[act(BF16), act(BF16)],
            compiler_params=cp, name=f"hyena_fwd{o}",
        )(*args, P, Q, KL, fc, fs)
        if first:
            zins, zargs = [zspec(0), wspec(0), bspec(0)], [z2d, cw, cbias]
        else:
            zins, zargs = [aspec], [zcur]
        zcur = pl.pallas_call(
            functools.partial(_hy_inv_kernel, first=first), grid=grid,
            in_specs=[aspec, aspec, dspec, dspec] + zins + [zspec(o + 1), wspec(o + 1), bspec(o + 1), kspec(o)],
            out_specs=aspec, out_shape=act(F32),
            compiler_params=cp, name=f"hyena_inv{o}",
        )(yr, yn, gc, gs, *zargs, z2d, cw, cbias, bias)
    return zcur


S5_BLK_STATES = (LANES // S5_GROUP_CH) * S5_STATE


def _s5_param_kernel(lr_ref, li_ref, ls_ref, br_ref, bi_ref, lbr_ref, lbi_ref, bbr_ref, bbi_ref):
    lr, li = lr_ref[...], li_ref[...]
    step = jnp.exp(ls_ref[...])
    mag = jnp.exp(lr * step)
    lbr = mag * jnp.cos(li * step)
    lbi = mag * jnp.sin(li * step)
    den = lr * lr + li * li
    nr = lbr - 1.0
    cr = (nr * lr + lbi * li) / den
    ci = (lbi * lr - nr * li) / den
    br, bi = br_ref[...], bi_ref[...]
    lbr_ref[...] = lbr
    lbi_ref[...] = lbi
    bbr_ref[...] = cr * br - ci * bi
    bbi_ref[...] = cr * bi + ci * br


def _s5_params(lp):
    n = 2 * S5_GROUPS * S5_STATE
    col = lambda a: a.reshape(n, 1)
    ls = jnp.broadcast_to(lp['s5_log_step'][:, :, None], (2, S5_GROUPS, S5_STATE))
    tr = 1024
    cspec = pl.BlockSpec((tr, 1), lambda i: (i, 0))
    bspec = pl.BlockSpec((tr, S5_GROUP_CH), lambda i: (i, 0))
    lbr, lbi, bbr, bbi = pl.pallas_call(
        _s5_param_kernel, grid=(n // tr,),
        in_specs=[cspec, cspec, cspec, bspec, bspec],
        out_specs=[cspec, cspec, bspec, bspec],
        out_shape=[jax.ShapeDtypeStruct((n, 1), F32)] * 2 + [jax.ShapeDtypeStruct((n, S5_GROUP_CH), F32)] * 2,
        compiler_params=_cparams(("parallel",)),
        name="s5_params",
    )(col(lp['s5_lambda_re']), col(lp['s5_lambda_im']), col(ls),
      lp['s5_B_re'].reshape(n, S5_GROUP_CH), lp['s5_B_im'].reshape(n, S5_GROUP_CH))
    gp = (2, S5_GROUPS, S5_STATE)
    return lbr.reshape(gp), lbi.reshape(gp), bbr.reshape(gp + (S5_GROUP_CH,)), bbi.reshape(gp + (S5_GROUP_CH,))


def _s5_block_weights(lbr, lbi, bbr, bbi, c_re, c_im):
    gl = LANES // S5_GROUP_CH
    nb = S5_GROUPS // gl
    eye = jnp.eye(gl, dtype=F32)

    def in_blk(bb):
        t = jnp.transpose(bb.reshape(2, nb, gl, S5_STATE, S5_GROUP_CH), (0, 1, 2, 4, 3))
        t = t[:, :, :, :, None, :] * eye[None, None, :, None, :, None]
        return t.reshape(2, nb, LANES, S5_BLK_STATES)

    def out_blk(c):
        t = jnp.transpose(c.reshape(2, nb, gl, S5_GROUP_CH, S5_STATE), (0, 1, 2, 4, 3))
        t = t[:, :, :, :, None, :] * eye[None, None, :, None, :, None]
        return t.reshape(2, nb, S5_BLK_STATES, LANES)

    w_in = jnp.concatenate([in_blk(bbr), in_blk(bbi)], axis=-1).astype(BF16)
    lam = jnp.stack([lbr.reshape(2, nb, S5_BLK_STATES), lbi.reshape(2, nb, S5_BLK_STATES)], axis=2)
    lam = jnp.transpose(lam, (1, 0, 2, 3)).reshape(nb, 4, S5_BLK_STATES)
    return w_in, lam, out_blk(c_re).astype(BF16), out_blk(c_im).astype(BF16)


def _s5_kernel(uf_ref, ub_ref, wf_ref, wb_ref, cfr_ref, cfi_ref, cbr_ref, cbi_ref, lam_ref, s0_ref,
               yf_ref, yb_ref, fin_ref, xf_scr, xb_scr, st_scr):
    tt, b, _ = uf_ref.shape
    ns = S5_BLK_STATES
    i = pl.program_id(1)

    @pl.when(i == 0)
    def _():
        st_scr[...] = s0_ref[0]

    xf_scr[...] = _dot(uf_ref[...].reshape(tt * b, LANES).astype(BF16), wf_ref[0, 0])
    xb_scr[...] = _dot(ub_ref[...].reshape(tt * b, LANES).astype(BF16), wb_ref[0, 0])
    lam = lam_ref[0]
    lfr, lfi, lbr, lbi = lam[0:1], lam[1:2], lam[2:3], lam[3:4]

    def step(t, carry):
        sfr, sfi, sbr, sbi = carry
        rf = pl.ds(pl.multiple_of(t * b, b), b)
        nfr = lfr * sfr - lfi * sfi + xf_scr[rf, :ns]
        nfi = lfr * sfi + lfi * sfr + xf_scr[rf, ns:]
        xf_scr[rf, :ns] = nfr
        xf_scr[rf, ns:] = nfi
        rb = pl.ds(pl.multiple_of((tt - 1 - t) * b, b), b)
        nbr = lbr * sbr - lbi * sbi + xb_scr[rb, :ns]
        nbi = lbr * sbi + lbi * sbr + xb_scr[rb, ns:]
        xb_scr[rb, :ns] = nbr
        xb_scr[rb, ns:] = nbi
        return nfr, nfi, nbr, nbi

    init = (st_scr[0, :, :ns], st_scr[0, :, ns:], st_scr[1, :, :ns], st_scr[1, :, ns:])
    sfr, sfi, sbr, sbi = lax.fori_loop(0, tt, step, init)
    st_scr[0, :, :ns] = sfr
    st_scr[0, :, ns:] = sfi
    st_scr[1, :, :ns] = sbr
    st_scr[1, :, ns:] = sbi

    yf = _dot(xf_scr[:, :ns].astype(BF16), cfr_ref[0, 0]) - _dot(xf_scr[:, ns:].astype(BF16), cfi_ref[0, 0])
    yb = _dot(xb_scr[:, :ns].astype(BF16), cbr_ref[0, 0]) - _dot(xb_scr[:, ns:].astype(BF16), cbi_ref[0, 0])
    yf_ref[...] = yf.reshape(tt, b, LANES)
    yb_ref[...] = yb.reshape(tt, b, LANES)

    @pl.when(i == pl.num_programs(1) - 1)
    def _():
        fin_ref[0] = st_scr[...]


def _s5(z3d, blk, s0, *, tt=64):
    L, B, _ = z3d.shape
    w_in, lam, c_re, c_im = blk
    nb = w_in.shape[1]
    tt = min(tt, L)
    nt = L // tt
    u0 = Z_S5_IN // LANES
    ns2 = 2 * S5_BLK_STATES
    wspec = lambda d: pl.BlockSpec((1, 1, LANES, ns2), lambda j, i: (d, j, 0, 0))
    cspec = lambda d: pl.BlockSpec((1, 1, S5_BLK_STATES, LANES), lambda j, i: (d, j, 0, 0))
    return pl.pallas_call(
        _s5_kernel, grid=(nb, nt),
        in_specs=[pl.BlockSpec((tt, B, LANES), lambda j, i: (i, 0, u0 + j)),
                  pl.BlockSpec((tt, B, LANES), lambda j, i: (nt - 1 - i, 0, u0 + j)),
                  wspec(0), wspec(1), cspec(0), cspec(0), cspec(1), cspec(1),
                  pl.BlockSpec((1, 4, S5_BLK_STATES), lambda j, i: (j, 0, 0)),
                  pl.BlockSpec((1, 2, B, ns2), lambda j, i: (j, 0, 0, 0))],
        out_specs=[pl.BlockSpec((tt, B, LANES), lambda j, i: (i, 0, j)),
                   pl.BlockSpec((tt, B, LANES), lambda j, i: (nt - 1 - i, 0, j)),
                   pl.BlockSpec((1, 2, B, ns2), lambda j, i: (j, 0, 0, 0))],
        out_shape=[jax.ShapeDtypeStruct((L, B, BRANCH_W), F32)] * 2
                  + [jax.ShapeDtypeStruct((nb, 2, B, ns2), F32)],
        scratch_shapes=[pltpu.VMEM((tt * B, ns2), F32), pltpu.VMEM((tt * B, ns2), F32),
                        pltpu.VMEM((2, B, ns2), F32)],
        compiler_params=_cparams(("parallel", "arbitrary")),
        name="s5_scan",
    )(z3d, z3d, w_in, w_in, c_re, c_im, c_re, c_im, lam, s0)


def _s5_state_to_blocks(s0):
    B = s0.shape[0]
    gl = LANES // S5_GROUP_CH
    nb = S5_GROUPS // gl
    t = s0.reshape(B, 2, nb, gl * S5_STATE, 2)
    t = jnp.transpose(t, (2, 1, 0, 4, 3))
    return t.reshape(nb, 2, B, 2 * S5_BLK_STATES)


def _s5_blocks_to_state(fin):
    nb, _, B, _ = fin.shape
    t = fin.reshape(nb, 2, B, 2, S5_BLK_STATES)
    t = jnp.transpose(t, (2, 1, 0, 4, 3))
    return t.reshape(B, 2, S5_GROUPS, S5_STATE, 2)


def _gdn_gate_kernel(ab_ref, alog_ref, dtb_ref, out_ref):
    L = ab_ref.shape[0]
    c = GDN_CHUNK
    lane = lax.broadcasted_iota(jnp.int32, (c, LANES), 1)
    r = lax.broadcasted_iota(jnp.int32, (c, c), 0)
    s = lax.broadcasted_iota(jnp.int32, (c, c), 1)
    lower = jnp.where(r >= s, 1.0, 0.0).astype(BF16)
    upper = jnp.where(r <= s, 1.0, 0.0).astype(BF16)
    neg_a = -jnp.exp(alog_ref[...])
    dtb = dtb_ref[...]

    def body(n, carry):
        rows = pl.ds(pl.multiple_of(n * c, c), c)
        x = ab_ref[rows, :]
        xa = x + dtb
        softplus = jnp.maximum(xa, 0.0) + jnp.log1p(jnp.exp(-jnp.abs(xa)))
        g = jnp.where((lane >= 2 * GDN_HEADS) & (lane < 4 * GDN_HEADS), neg_a * softplus, 0.0)
        g1 = g.astype(BF16)
        r1 = g - g1.astype(F32)
        g2 = r1.astype(BF16)
        g3 = (r1 - g2.astype(F32)).astype(BF16)
        cf = _dot(lower, g1) + _dot(lower, g2) + _dot(lower, g3)
        cb = _dot(upper, g1) + _dot(upper, g2) + _dot(upper, g3)
        gc = jnp.where(lane < 3 * GDN_HEADS, cf, cb)
        out_ref[rows, :] = jnp.where(lane < 2 * GDN_HEADS, jax.nn.sigmoid(x), gc)
        return carry

    lax.fori_loop(0, L // c, body, 0)


def _gdn_gates(z2d, B, lp):
    L = z2d.shape[0]
    ab0 = Z_AB // LANES
    zb = NZ // LANES
    pad = lambda a: jnp.zeros((1, LANES), F32).at[0, 2 * GDN_HEADS:4 * GDN_HEADS].set(a.reshape(-1))
    return pl.pallas_call(
        _gdn_gate_kernel, grid=(B,),
        in_specs=[pl.BlockSpec((L, LANES), lambda b: (0, b * zb + ab0)),
                  pl.BlockSpec((1, LANES), lambda b: (0, 0)),
                  pl.BlockSpec((1, LANES), lambda b: (0, 0))],
        out_specs=pl.BlockSpec((L, LANES), lambda b: (0, b)),
        out_shape=jax.ShapeDtypeStruct((L, B * LANES), F32),
        compiler_params=_cparams(("parallel",)),
        name="gdn_gates",
    )(z2d, pad(lp['gdn_A_log']), pad(lp['gdn_dt_bias']))


def _l2norm(x):
    return x * lax.rsqrt(jnp.sum(x * x, axis=-1, keepdims=True) + NORM_EPS)


def _chunk_masks():
    c = GDN_CHUNK
    r = lax.broadcasted_iota(jnp.int32, (c, c), 0)
    s = lax.broadcasted_iota(jnp.int32, (c, c), 1)
    return ((r >= s, r > s), (r <= s, r < s))


def _chunk_decay(col_ref, row_ref, rows, ci, d, incl):
    col = col_ref[0, 0, rows, :]
    beta = col[:, d:d + 1]
    gcum = col[:, 2 + d:3 + d]
    grow = row_ref[0, 0, pl.ds(ci, 1), 2 + d, :]
    decay = jnp.where(incl, jnp.exp(jnp.where(incl, gcum - grow, 0.0)), 0.0)
    return beta, gcum, decay


def _gdn_amat_kernel(k_ref, wk_ref, col_ref, row_ref, a_ref, k_scr):
    c = GDN_CHUNK
    nc = k_ref.shape[0] // c
    k_scr[...] = _l2norm(_silu(_dwconv3(k_ref[...], wk_ref[...])))
    masks = _chunk_masks()

    def body(n, carry):
        rows = pl.ds(pl.multiple_of(n * c, c), c)
        kc = k_scr[rows, :]
        for d in range(2):
            incl, strict = masks[d]
            beta, _, decay = _chunk_decay(col_ref, row_ref, rows, n, d, incl)
            kk = lax.dot_general((kc * beta).astype(BF16), kc.astype(BF16),
                                 (((1,), (1,)), ((), ())), preferred_element_type=F32)
            a_ref[0, 0, d, pl.ds(n, 1)] = jnp.where(strict, kk * decay, 0.0)[None]
        return carry

    lax.fori_loop(0, nc, body, 0)


def _tri_inverse_kernel(a_ref, t_ref, *, upper):
    n, _, pw = a_ref.shape
    col = lax.broadcasted_iota(jnp.int32, (n, pw), 0)
    for i in (range(n - 1, -1, -1) if upper else range(n)):
        lo, hi = (i + 1, n) if upper else (0, i)

        def inner(j, acc, i=i):
            return acc - a_ref[i, pl.ds(j, 1), :] * t_ref[j]

        t_ref[i] = lax.fori_loop(lo, hi, inner, jnp.where(col == i, 1.0, 0.0))


def _tri_inverse(a, *, upper, pw=LANES):
    c, _, P = a.shape
    pp = -(-P // pw) * pw
    if pp != P:
        a = jnp.pad(a, ((0, 0), (0, 0), (0, pp - P)))
    spec = pl.BlockSpec((c, c, pw), lambda p: (0, 0, p))
    t = pl.pallas_call(
        functools.partial(_tri_inverse_kernel, upper=upper), grid=(pp // pw,),
        in_specs=[spec], out_specs=spec, out_shape=jax.ShapeDtypeStruct((c, c, pp), F32),
        compiler_params=_cparams(("parallel",)),
        name="gdn_tri_inverse",
    )(a)
    return t[:, :, :P]


def _gdn_kernel(q_ref, k_ref, v_ref, wq_ref, wk_ref, wv_ref, col_ref, row_ref, t_ref, ng_ref, s0_ref,
                o_ref, sfin_ref, q_scr, k_scr, v_scr, ob_scr, st_scr):
    L = q_ref.shape[0]
    c = GDN_CHUNK
    nc = L // c
    q_scr[...] = _l2norm(_silu(_dwconv3(q_ref[...], wq_ref[...]))) * (GDN_DK ** -0.5)
    k_scr[...] = _l2norm(_silu(_dwconv3(k_ref[...], wk_ref[...])))
    v_scr[...] = _silu(_dwconv3(v_ref[...], wv_ref[...]))
    st_scr[...] = s0_ref[0, :, 0]
    masks = _chunk_masks()

    def chunk(ci, d):
        rows = pl.ds(pl.multiple_of(ci * c, c), c)
        qc, kc, vc = q_scr[rows, :], k_scr[rows, :], v_scr[rows, :]
        beta, gcum, decay = _chunk_decay(col_ref, row_ref, rows, ci, d, masks[d][0])
        qk = decay * lax.dot_general(qc.astype(BF16), kc.astype(BF16),
                                     (((1,), (1,)), ((), ())), preferred_element_type=F32)
        t_mat = t_ref[0, 0, d, pl.ds(ci, 1)][0]
        egc = jnp.exp(gcum)
        uw = _bdot(t_mat, jnp.concatenate([vc * beta, kc * (beta * egc)], axis=1))
        u, w = uw[:, :GDN_DV], uw[:, GDN_DV:]
        state = st_scr[d]
        ws = _bdot(jnp.concatenate([w, qc * egc], axis=0), state)
        v_new = u - ws[:c]
        o = ws[c:] + _bdot(qk, v_new)
        g_last = gcum[c - 1:c] if d == 0 else gcum[0:1]
        kdec = kc * jnp.exp(g_last - gcum)
        st_scr[d] = state * jnp.exp(g_last) + lax.dot_general(
            kdec.astype(BF16), v_new.astype(BF16), (((0,), (0,)), ((), ())), preferred_element_type=F32)
        return rows, o

    def body(n, carry):
        rows, o = chunk(n, 0)
        o_ref[rows, :] = o
        rows, o = chunk(nc - 1 - n, 1)
        ob_scr[rows, :] = o
        return carry

    lax.fori_loop(0, nc, body, 0)
    o = o_ref[...] + ob_scr[...]
    o_ref[...] = o * lax.rsqrt(jnp.mean(o * o, axis=-1, keepdims=True) + NORM_EPS) * ng_ref[...]
    sfin_ref[0, :, 0] = st_scr[...]


def _gdn(z2d, B, lp, s0):
    L = z2d.shape[0]
    H = GDN_HEADS
    c = GDN_CHUNK
    nc = L // c
    bg = _gdn_gates(z2d, B, lp).reshape(L, B, LANES)[:, :, :4 * H].reshape(L, B, 4, H)
    cols = jnp.transpose(bg, (1, 3, 0, 2))
    rows = jnp.transpose(bg.reshape(nc, c, B, 4, H), (2, 4, 0, 3, 1))
    zb = NZ // LANES
    q0 = Z_GDN_IN // LANES
    zspec = lambda part: pl.BlockSpec((L, LANES), lambda b, h: (0, b * zb + q0 + part * H + h))
    wspec = lambda part: pl.BlockSpec((3, LANES), lambda b, h: (0, part * H + h))
    colspec = pl.BlockSpec((1, 1, L, 4), lambda b, h: (b, h, 0, 0))
    rowspec = pl.BlockSpec((1, 1, nc, 4, c), lambda b, h: (b, h, 0, 0, 0))
    tspec = pl.BlockSpec((1, 1, 2, nc, c, c), lambda b, h: (b, h, 0, 0, 0, 0))
    sspec = pl.BlockSpec((1, 2, 1, GDN_DK, GDN_DV), lambda b, h: (b, 0, h, 0, 0))
    cw = lp['gdn_conv_w']
    cp = _cparams(("parallel", "parallel"))
    a_mat = pl.pallas_call(
        _gdn_amat_kernel, grid=(B, H),
        in_specs=[zspec(1), wspec(1), colspec, rowspec],
        out_specs=tspec, out_shape=jax.ShapeDtypeStruct((B, H, 2, nc, c, c), F32),
        scratch_shapes=[pltpu.VMEM((L, LANES), F32)],
        compiler_params=cp, name="gdn_amat",
    )(z2d, cw, cols, rows)
    a_t = jnp.transpose(a_mat, (2, 4, 5, 0, 1, 3)).reshape(2, c, c, B * H * nc)
    t_t = jnp.stack([_tri_inverse(a_t[0], upper=False), _tri_inverse(a_t[1], upper=True)])
    t_mat = jnp.transpose(t_t.reshape(2, c, c, B, H, nc), (3, 4, 0, 5, 1, 2))
    return pl.pallas_call(
        _gdn_kernel, grid=(B, H),
        in_specs=[zspec(0), zspec(1), zspec(2), wspec(0), wspec(1), wspec(2), colspec, rowspec, tspec,
                  pl.BlockSpec((1, LANES), lambda b, h: (0, 0)), sspec],
        out_specs=[pl.BlockSpec((L, LANES), lambda b, h: (0, b * H + h)), sspec],
        out_shape=[jax.ShapeDtypeStruct((L, B * BRANCH_W), F32),
                   jax.ShapeDtypeStruct((B, 2, H, GDN_DK, GDN_DV), F32)],
        scratch_shapes=[pltpu.VMEM((L, LANES), F32)] * 4 + [pltpu.VMEM((2, GDN_DK, GDN_DV), F32)],
        compiler_params=cp, name="gdn",
    )(z2d, z2d, z2d, cw, cw, cw, cols, rows, t_mat, lp['gdn_norm_g'].reshape(1, LANES), s0)


def _merge_kernel(x_ref, hy_ref, yf_ref, yb_ref, og_ref, hyg_ref, u_ref, s5g_ref, gdg_ref,
                  m0_ref, m1_ref, m2_ref, d_ref, gw_ref, gb_ref, wb_ref, wo_ref, gate_ref, fg_ref,
                  out_ref, *, final_norm):
    tl, b, d = x_ref.shape
    rows = tl * b
    flat = lambda ref: ref[...].reshape(rows, ref.shape[-1])
    y_hy = flat(hy_ref) * _silu(flat(hyg_ref))
    s5_raw = d_ref[...] * flat(u_ref) + flat(yf_ref) + flat(yb_ref)
    gelu = 0.5 * s5_raw * (1.0 + jnp.tanh(math.sqrt(2.0 / math.pi) * (s5_raw + 0.044715 * (s5_raw * s5_raw * s5_raw))))
    glu = _dot(gelu.astype(BF16), gw_ref[...]) + gb_ref[...]
    y_s5 = glu[:, :BRANCH_W] * jax.nn.sigmoid(glu[:, BRANCH_W:]) * _silu(flat(s5g_ref))
    y_gdn = flat(og_ref) * _silu(flat(gdg_ref))
    merged = jax.nn.sigmoid(flat(m0_ref)) * _dot(y_hy.astype(BF16), wb_ref[0])
    merged += jax.nn.sigmoid(flat(m1_ref)) * _dot(y_s5.astype(BF16), wb_ref[1])
    merged += jax.nn.sigmoid(flat(m2_ref)) * _dot(y_gdn.astype(BF16), wb_ref[2])
    delta = _dot(merged.astype(BF16), wo_ref[...]).reshape(tl, b, d)
    x = x_ref[...] + gate_ref[...] * delta
    if final_norm:
        x = x * lax.rsqrt(jnp.mean(x * x, axis=-1, keepdims=True) + NORM_EPS) * fg_ref[...]
    out_ref[...] = x


def _merge(x, hy, yf, yb, og, z3d, gate, lp, final_g, *, rows_per_tile=256):
    L, B, _ = x.shape
    tl = max(1, min(L, rows_per_tile // B))
    act = pl.BlockSpec((tl, B, BRANCH_W), lambda i: (i, 0, 0))
    zspec = lambda col: pl.BlockSpec((tl, B, BRANCH_W), lambda i: (i, 0, col // BRANCH_W))
    once = pl.Buffered(1)
    const = lambda *shape: pl.BlockSpec(shape, lambda i: (0,) * len(shape), pipeline_mode=once)
    return pl.pallas_call(
        functools.partial(_merge_kernel, final_norm=final_g is not None),
        grid=(L // tl,),
        in_specs=[act, act, act, act, act,
                  zspec(Z_HY_GATE), zspec(Z_S5_IN), zspec(Z_S5_GATE), zspec(Z_GDN_GATE),
                  zspec(Z_MERGE), zspec(Z_MERGE + D_MODEL), zspec(Z_MERGE + 2 * D_MODEL),
                  const(1, BRANCH_W), const(BRANCH_W, 2 * BRANCH_W), const(1, 2 * BRANCH_W),
                  const(3, BRANCH_W, D_MODEL), const(D_MODEL, D_MODEL),
                  const(1, B, D_MODEL), const(1, 1, D_MODEL)],
        out_specs=act,
        out_shape=jax.ShapeDtypeStruct((L, B, D_MODEL), F32),
        compiler_params=_cparams(("parallel",)),
        name="merge",
    )(x, hy, yf, yb, og, z3d, z3d, z3d, z3d, z3d, z3d, z3d,
      lp['s5_D'].reshape(1, BRANCH_W), lp['s5_glu_w'].astype(BF16), lp['s5_glu_b'].reshape(1, -1),
      lp['w_branch'].astype(BF16), lp['w_out'].astype(BF16), gate[None],
      (final_g if final_g is not None else jnp.ones((D_MODEL,), F32)).reshape(1, 1, D_MODEL))


def _pack_w_in(w_in):
    pad = jnp.zeros((D_MODEL, NZ - Z_AB - 4 * GDN_HEADS), w_in.dtype)
    ab = w_in[:, W_IN_AB:W_IN_AB + 4 * GDN_HEADS]
    return jnp.concatenate([w_in[:, :W_IN_AB], w_in[:, W_IN_AB + 4 * GDN_HEADS:], ab, pad], axis=1).astype(BF16)


def _trunk_layer(x, pos, mod, lp, shared, s0_gdn, s0_s5, final_g):
    L, B, _ = x.shape
    mod3 = jnp.transpose(mod.reshape(B, 3, D_MODEL), (1, 0, 2))
    z3d, x = _in_projection(x, pos, mod3, lp['norm_g'], shared['w_packed'])
    z2d = z3d.reshape(L, B * NZ)
    tables = shared['tables'][L]
    spectra = _hyena_filter_spectra(L, lp, (tables['zpos'], tables['decay'], tables['dft'][0], tables['dft'][1]))
    hy = _hyena(z2d, B, lp, spectra, tables['dft'])
    yf, yb, s5_fin = _s5(z3d, shared['s5_blk'], _s5_state_to_blocks(s0_s5))
    og, gdn_fin = _gdn(z2d, B, lp, s0_gdn)
    x = _merge(x, hy.reshape(L, B, BRANCH_W), yf, yb, og.reshape(L, B, BRANCH_W), z3d, mod3[2], lp, final_g)
    return x, gdn_fin, _s5_blocks_to_state(s5_fin)


def _grid_pos_embed(n_tokens, dim):
    rows = n_tokens // GRID_W
    t = jnp.arange(rows * GRID_W)
    r = (t // GRID_W).astype(F32)
    col = (t % GRID_W).astype(F32)
    quarter = dim // 4
    omega = 1.0 / (10000.0 ** (jnp.arange(quarter, dtype=F32) / quarter))
    er = r[:, None] * omega
    ec = col[:, None] * omega
    return jnp.concatenate([jnp.sin(er), jnp.cos(er), jnp.sin(ec), jnp.cos(ec)], axis=-1)


_LAYER_KEYS = ('norm_g', 'w_in', 'hy_conv_w', 'hy_conv_b', 'hy_f_w1', 'hy_f_b1', 'hy_f_w2', 'hy_f_b2',
               'hy_f_w3', 'hy_f_freq', 'hy_bias', 's5_lambda_re', 's5_lambda_im', 's5_log_step',
               's5_B_re', 's5_B_im', 's5_C_re', 's5_C_im', 's5_D', 's5_glu_w', 's5_glu_b', 'gdn_conv_w',
               'gdn_A_log', 'gdn_dt_bias', 'gdn_norm_g', 'w_branch', 'w_out')


def kernel(x_prompt, x_sample, c, c_ctx, state_gdn, state_s5, w_mod, b_mod, norm_g, w_in, hy_conv_w, hy_conv_b, hy_f_w1, hy_f_b1, hy_f_w2, hy_f_b2, hy_f_w3, hy_f_freq, hy_bias, s5_lambda_re, s5_lambda_im, s5_log_step, s5_B_re, s5_B_im, s5_C_re, s5_C_im, s5_D, s5_glu_w, s5_glu_b, gdn_conv_w, gdn_A_log, gdn_dt_bias, gdn_norm_g, w_branch, w_out, final_norm_g):
    params = dict(norm_g=norm_g, w_in=w_in, hy_conv_w=hy_conv_w, hy_conv_b=hy_conv_b, hy_f_w1=hy_f_w1,
                  hy_f_b1=hy_f_b1, hy_f_w2=hy_f_w2, hy_f_b2=hy_f_b2, hy_f_w3=hy_f_w3, hy_f_freq=hy_f_freq,
                  hy_bias=hy_bias, s5_lambda_re=s5_lambda_re, s5_lambda_im=s5_lambda_im,
                  s5_log_step=s5_log_step, s5_B_re=s5_B_re, s5_B_im=s5_B_im, s5_C_re=s5_C_re,
                  s5_C_im=s5_C_im, s5_D=s5_D, s5_glu_w=s5_glu_w, s5_glu_b=s5_glu_b, gdn_conv_w=gdn_conv_w,
                  gdn_A_log=gdn_A_log, gdn_dt_bias=gdn_dt_bias, gdn_norm_g=gdn_norm_g, w_branch=w_branch,
                  w_out=w_out)
    depth = w_in.shape[0]
    bp, lc, _ = x_prompt.shape
    bl, ll, _ = x_sample.shape

    cc = jnp.concatenate([c_ctx[None], c], axis=0)
    n_rows = -(-cc.shape[0] // 8) * 8
    cc = jnp.pad(cc, ((0, n_rows - cc.shape[0]), (0, 0)))
    mods = _modulation(cc, w_mod, b_mod)

    tables = {}
    for L in {lc, ll}:
        zpos, decay = _filter_tables(L)
        tables[L] = dict(zpos=zpos, decay=decay, dft=_dft_tables(L))

    xc = jnp.transpose(x_prompt, (1, 0, 2))
    xl = jnp.transpose(x_sample, (1, 0, 2))
    pos = _grid_pos_embed(ll, D_MODEL).astype(x_sample.dtype).reshape(ll, 1, D_MODEL)
    zero_gdn = jnp.zeros((bp, 2, GDN_HEADS, GDN_DK, GDN_DV), F32)
    zero_s5 = jnp.zeros((bp, 2, S5_GROUPS, S5_STATE, 2), F32)
    gdn_states, s5_states = [], []
    for l in range(depth):
        lp = {k: params[k][l] for k in _LAYER_KEYS}
        lbr, lbi, bbr, bbi = _s5_params(lp)
        shared = dict(w_packed=_pack_w_in(lp['w_in']), tables=tables,
                      s5_blk=_s5_block_weights(lbr, lbi, bbr, bbi, lp['s5_C_re'], lp['s5_C_im']))
        fg = final_norm_g if l == depth - 1 else None
        mod_ctx = jnp.broadcast_to(mods[l, 0:1], (bp, 3 * D_MODEL))
        mod_lat = mods[l, 1:1 + bl]
        xc, fin_gdn, fin_s5 = _trunk_layer(xc, None, mod_ctx, lp, shared, zero_gdn, zero_s5, fg)
        gdn_states.append(fin_gdn)
        s5_states.append(fin_s5)
        xl, _, _ = _trunk_layer(xl, pos if l == 0 else None, mod_lat, lp, shared,
                                state_gdn[:, l].astype(F32), state_s5[:, l].astype(F32), fg)
    y_prompt = jnp.transpose(xc, (1, 0, 2))
    y_sample = jnp.transpose(xl, (1, 0, 2))
    return (y_prompt, y_sample, jnp.stack(gdn_states, axis=1), jnp.stack(s5_states, axis=1))
```

```python
import functools
import math

import jax
import jax.numpy as jnp
from jax import lax
from jax.experimental import pallas as pl
from jax.experimental.pallas import tpu as pltpu

F32 = jnp.float32
BF16 = jnp.bfloat16

D_MODEL = 1024
BRANCH_W = 1024
GRID_W = 64
HY_ORDER = 2
HY_EMB = 33
HY_HID = 64
HY_DECAY_TARGET = 1e-2
HY_SHORT_DECAY_PCT = 0.3
HY_LONG_DECAY_PCT = 1.5
S5_GROUP_CH = 16
S5_GROUPS = 64
S5_STATE = 64
GDN_DK = 128
GDN_DV = 128
GDN_HEADS = 8
GDN_CHUNK = 64
NORM_EPS = 1e-6

LANES = 128
Z_HY_IN, Z_HY_GATE, Z_S5_IN, Z_S5_GATE = 0, 3072, 4096, 5120
Z_GDN_IN, Z_GDN_GATE, Z_MERGE, Z_AB = 6144, 9216, 10240, 13312
NZ = 14336
W_IN_AB = 9216
VMEM_LIMIT_V7X = 56 * 1024 * 1024


def _cparams(sem):
    return pltpu.CompilerParams(dimension_semantics=sem, vmem_limit_bytes=VMEM_LIMIT_V7X)


def _dot(a, b):
    return jnp.dot(a, b, preferred_element_type=F32)


def _bdot(a, b):
    return jnp.dot(a.astype(BF16), b.astype(BF16), preferred_element_type=F32)


def _split(a):
    hi = a.astype(BF16)
    lo = (a - hi.astype(F32)).astype(BF16)
    return hi, lo


def _dot3(a, b):
    ah, al = _split(a)
    bh, bl = _split(b)
    return _dot(ah, bh) + _dot(al, bh) + _dot(ah, bl)


def _silu(x):
    return x * jax.nn.sigmoid(x)


def _shift_prev(x):
    r = lax.broadcasted_iota(jnp.int32, x.shape, 0)
    return jnp.where(r == 0, 0.0, pltpu.roll(x, 1, axis=0))


def _shift_next(x):
    n = x.shape[0]
    r = lax.broadcasted_iota(jnp.int32, x.shape, 0)
    return jnp.where(r == n - 1, 0.0, pltpu.roll(x, n - 1, axis=0))


def _dwconv3(x, w):
    return w[0:1] * _shift_prev(x) + w[1:2] * x + w[2:3] * _shift_next(x)


def _mod_kernel(c_ref, w_ref, b_ref, o_ref):
    o_ref[0] = _dot3(_silu(c_ref[...]), w_ref[0]) + b_ref[0]


def _modulation(cc, w_mod, b_mod):
    depth = w_mod.shape[0]
    r = cc.shape[0]
    return pl.pallas_call(
        _mod_kernel,
        grid=(depth, 3),
        in_specs=[pl.BlockSpec((r, D_MODEL), lambda l, n: (0, 0)),
                  pl.BlockSpec((1, D_MODEL, D_MODEL), lambda l, n: (l, 0, n)),
                  pl.BlockSpec((1, 1, D_MODEL), lambda l, n: (l, 0, n))],
        out_specs=pl.BlockSpec((1, r, D_MODEL), lambda l, n: (l, 0, n)),
        out_shape=jax.ShapeDtypeStruct((depth, r, 3 * D_MODEL), F32),
        compiler_params=_cparams(("arbitrary", "arbitrary")),
        name="modulation",
    )(cc, w_mod, b_mod.reshape(depth, 1, 3 * D_MODEL))


def _inproj_kernel(*refs, has_pos):
    if has_pos:
        x_ref, pos_ref, mod_ref, g_ref, w_ref, z_ref, xres_ref, h_scr = refs
    else:
        x_ref, mod_ref, g_ref, w_ref, z_ref, h_scr = refs
    tl, b, d = x_ref.shape

    @pl.when(pl.program_id(1) == 0)
    def _():
        x = x_ref[...]
        if has_pos:
            x = x + pos_ref[...]
            xres_ref[...] = x
        y = x * lax.rsqrt(jnp.mean(x * x, axis=-1, keepdims=True) + NORM_EPS) * g_ref[...]
        h = y * (1.0 + mod_ref[1]) + mod_ref[0]
        h_scr[...] = h.reshape(tl * b, d).astype(BF16)

    z = _dot(h_scr[...], w_ref[...])
    z_ref[...] = z.reshape(tl, b, z.shape[-1])


def _in_projection(x, pos, mod3, norm_g, w_packed, *, rows_per_tile=1024, tn=2048):
    L, B, _ = x.shape
    tl = min(L, rows_per_tile // B)
    grid = (L // tl, NZ // tn)
    has_pos = pos is not None
    in_specs = [pl.BlockSpec((tl, B, D_MODEL), lambda i, j: (i, 0, 0))]
    args = [x]
    if has_pos:
        in_specs.append(pl.BlockSpec((tl, 1, D_MODEL), lambda i, j: (i, 0, 0)))
        args.append(pos)
    in_specs += [pl.BlockSpec((3, B, D_MODEL), lambda i, j: (0, 0, 0)),
                 pl.BlockSpec((1, 1, D_MODEL), lambda i, j: (0, 0, 0)),
                 pl.BlockSpec((D_MODEL, tn), lambda i, j: (0, j))]
    args += [mod3, norm_g.reshape(1, 1, D_MODEL), w_packed]
    out_specs = [pl.BlockSpec((tl, B, tn), lambda i, j: (i, 0, j))]
    out_shape = [jax.ShapeDtypeStruct((L, B, NZ), F32)]
    if has_pos:
        out_specs.append(pl.BlockSpec((tl, B, D_MODEL), lambda i, j: (i, 0, 0)))
        out_shape.append(jax.ShapeDtypeStruct((L, B, D_MODEL), F32))
    res = pl.pallas_call(
        functools.partial(_inproj_kernel, has_pos=has_pos),
        grid=grid, in_specs=in_specs, out_specs=out_specs, out_shape=out_shape,
        scratch_shapes=[pltpu.VMEM((tl * B, D_MODEL), BF16)],
        compiler_params=_cparams(("parallel", "arbitrary")),
        name="in_projection",
    )(*args)
    return (res[0], res[1]) if has_pos else (res[0], x)


def _dft_tables(L):
    f = lax.broadcasted_iota(jnp.int32, (L, L), 0)
    t = lax.broadcasted_iota(jnp.int32, (L, L), 1)
    ang = ((f * t) % (2 * L)).astype(F32) * (math.pi / L)
    c = jnp.cos(ang)
    s = jnp.sin(ang)
    alt = jnp.where(t % 2 == 0, 1.0, -1.0).astype(F32)
    fwd_c = c
    fwd_s = jnp.where(f == 0, alt, s)
    fi, ti = t, f
    inv_c = jnp.where(fi == 0, 1.0, 2.0) * c / (2.0 * L)
    inv_s = jnp.where(fi == 0, jnp.where(ti % 2 == 0, 1.0, -1.0), 2.0 * s) / (2.0 * L)
    return fwd_c.astype(BF16), fwd_s.astype(BF16), inv_c.astype(BF16), inv_s.astype(BF16)


def _filter_tables(L):
    t = jnp.linspace(0.0, 1.0, L, dtype=F32)[:, None]
    bands = (HY_EMB - 1) // 2
    fb = jnp.linspace(1e-4, bands - 1, bands, dtype=F32)[None, :]
    wpos = 2.0 * math.pi * jnp.arange(L, dtype=F32)[:, None] / L
    zpos = jnp.concatenate([t, jnp.cos(fb * wpos), -jnp.sin(fb * wpos)], axis=-1)
    deltas = jnp.abs(jnp.linspace(math.log(HY_DECAY_TARGET) / HY_LONG_DECAY_PCT,
                                  math.log(HY_DECAY_TARGET) / HY_SHORT_DECAY_PCT, BRANCH_W, dtype=F32))
    decay = jnp.exp(-t * deltas)
    return zpos, decay


def _hy_filter_kernel(zpos_ref, w1_ref, b1_ref, w2_ref, b2_ref, freq_ref, w3_ref, decay_ref,
                      fc_ref, fs_ref, p_ref, q_ref, kl_ref):
    freq = freq_ref[...]
    hdn = jnp.sin(freq * (_dot3(zpos_ref[...], w1_ref[...]) + b1_ref[...]))
    hdn = jnp.sin(freq * (_dot3(hdn, w2_ref[...]) + b2_ref[...]))
    decay = decay_ref[...]
    row = lax.broadcasted_iota(jnp.int32, decay.shape, 0)
    alt = jnp.where(row % 2 == 0, 1.0, -1.0)
    for o in range(HY_ORDER):
        fwd = _dot3(hdn, w3_ref[2 * o]) * decay
        bwd = jnp.where(row == 0, 0.0, _dot3(hdn, w3_ref[2 * o + 1]) * decay)
        norm = jnp.sum(jnp.abs(fwd) + jnp.abs(bwd), axis=0, keepdims=True)
        a = (fwd + bwd) / norm
        s = (fwd - bwd) / norm
        p_ref[o] = _bdot(fc_ref[...], a)
        q_ref[o] = jnp.where(row == 0, 0.0, _bdot(fs_ref[...], s))
        kl_ref[o] = jnp.sum(alt * a, axis=0, keepdims=True)


def _hyena_filter_spectra(L, lp, tables, *, cb=256):
    zpos, decay, fc, fs = tables
    w3 = jnp.transpose(lp['hy_f_w3'].reshape(HY_HID, HY_ORDER * 2, BRANCH_W), (1, 0, 2))
    const = lambda *shape: pl.BlockSpec(shape, lambda j: (0,) * len(shape))
    return pl.pallas_call(
        _hy_filter_kernel,
        grid=(BRANCH_W // cb,),
        in_specs=[const(L, HY_EMB), const(HY_EMB, HY_HID), const(1, HY_HID), const(HY_HID, HY_HID),
                  const(1, HY_HID), const(1, HY_HID),
                  pl.BlockSpec((HY_ORDER * 2, HY_HID, cb), lambda j: (0, 0, j)),
                  pl.BlockSpec((L, cb), lambda j: (0, j)),
                  const(L, L), const(L, L)],
        out_specs=[pl.BlockSpec((HY_ORDER, L, cb), lambda j: (0, 0, j)),
                   pl.BlockSpec((HY_ORDER, L, cb), lambda j: (0, 0, j)),
                   pl.BlockSpec((HY_ORDER, 1, cb), lambda j: (0, 0, j))],
        out_shape=[jax.ShapeDtypeStruct((HY_ORDER, L, BRANCH_W), F32),
                   jax.ShapeDtypeStruct((HY_ORDER, L, BRANCH_W), F32),
                   jax.ShapeDtypeStruct((HY_ORDER, 1, BRANCH_W), F32)],
        compiler_params=_cparams(("arbitrary",)),
        name="hyena_filter",
    )(zpos, lp['hy_f_w1'], lp['hy_f_b1'].reshape(1, HY_HID), lp['hy_f_w2'], lp['hy_f_b2'].reshape(1, HY_HID),
      lp['hy_f_freq'].reshape(1, HY_HID), w3, decay, fc, fs)


def _hy_fwd_kernel(*refs, first):
    if first:
        zin_ref, cw_ref, cbias_ref, p_ref, q_ref, kl_ref, fc_ref, fs_ref, yr_ref, yn_ref = refs
        zin = _dwconv3(zin_ref[...], cw_ref[...]) + cbias_ref[...]
    else:
        zin_ref, p_ref, q_ref, kl_ref, fc_ref, fs_ref, yr_ref, yn_ref = refs
        zin = zin_ref[...]
    zb = zin.astype(BF16)
    zr = _dot(fc_ref[...], zb)
    w = _dot(fs_ref[...], zb)
    p = p_ref[0]
    q = q_ref[0]
    row = lax.broadcasted_iota(jnp.int32, p.shape, 0)
    pd = jnp.where(row == 0, kl_ref[0], p)
    yr_ref[...] = (zr * p - w * q).astype(BF16)
    yn_ref[...] = (zr * q + w * pd).astype(BF16)


def _hy_inv_kernel(*refs, first):
    if first:
        (yr_ref, yn_ref, gc_ref, gs_ref, zin_ref, cwz_ref, cbz_ref, xg_ref, cwx_ref, cbx_ref,
         bias_ref, out_ref) = refs
        zin = _dwconv3(zin_ref[...], cwz_ref[...]) + cbz_ref[...]
    else:
        yr_ref, yn_ref, gc_ref, gs_ref, zin_ref, xg_ref, cwx_ref, cbx_ref, bias_ref, out_ref = refs
        zin = zin_ref[...]
    conv = _dot(gc_ref[...], yr_ref[...]) + _dot(gs_ref[...], yn_ref[...])
    xg = _dwconv3(xg_ref[...], cwx_ref[...]) + cbx_ref[...]
    out_ref[...] = xg * (conv + bias_ref[0] * zin)


def _hyena(z2d, B, lp, spectra, dft, *, cb=256):
    L = z2d.shape[0]
    P, Q, KL = spectra
    fc, fs, gc, gs = dft
    ncb = BRANCH_W // cb
    zb = NZ // cb
    cw = lp['hy_conv_w']
    cbias = lp['hy_conv_b'].reshape(1, -1)
    bias = lp['hy_bias'].reshape(HY_ORDER, 1, BRANCH_W)
    grid = (ncb, B)
    zspec = lambda part: pl.BlockSpec((L, cb), lambda j, b: (0, b * zb + part * ncb + j))
    wspec = lambda part: pl.BlockSpec((3, cb), lambda j, b: (0, part * ncb + j))
    bspec = lambda part: pl.BlockSpec((1, cb), lambda j, b: (0, part * ncb + j))
    aspec = pl.BlockSpec((L, cb), lambda j, b: (0, b * ncb + j))
    dspec = pl.BlockSpec((L, L), lambda j, b: (0, 0))
    ospec = lambda o: pl.BlockSpec((1, L, cb), lambda j, b: (o, 0, j))
    kspec = lambda o: pl.BlockSpec((1, 1, cb), lambda j, b: (o, 0, j))
    cp = _cparams(("parallel", "arbitrary"))
    act = lambda dt: jax.ShapeDtypeStruct((L, B * BRANCH_W), dt)

    zcur = None
    for o in range(HY_ORDER):
        first = o == 0
        if first:
            ins, args = [zspec(0), wspec(0), bspec(0)], [z2d, cw, cbias]
        else:
            ins, args = [aspec], [zcur]
        yr, yn = pl.pallas_call(
            functools.partial(_hy_fwd_kernel, first=first), grid=grid,
            in_specs=ins + [ospec(o), ospec(o), kspec(o), dspec, dspec],
            out_specs=[aspec, aspec], out_shape=[act(BF16), act(BF16)],
            compiler_params=cp, name=f"hyena_fwd{o}",
        )(*args, P, Q, KL, fc, fs)
        if first:
            zins, zargs = [zspec(0), wspec(0), bspec(0)], [z2d, cw, cbias]
        else:
            zins, zargs = [aspec], [zcur]
        zcur = pl.pallas_call(
            functools.partial(_hy_inv_kernel, first=first), grid=grid,
            in_specs=[aspec, aspec, dspec, dspec] + zins + [zspec(o + 1), wspec(o + 1), bspec(o + 1), kspec(o)],
            out_specs=aspec, out_shape=act(F32),
            compiler_params=cp, name=f"hyena_inv{o}",
        )(yr, yn, gc, gs, *zargs, z2d, cw, cbias, bias)
    return zcur


S5_BLK_STATES = (LANES // S5_GROUP_CH) * S5_STATE


def _s5_param_kernel(lr_ref, li_ref, ls_ref, br_ref, bi_ref, lbr_ref, lbi_ref, bbr_ref, bbi_ref):
    lr, li = lr_ref[...], li_ref[...]
    step = jnp.exp(ls_ref[...])
    mag = jnp.exp(lr * step)
    lbr = mag * jnp.cos(li * step)
    lbi = mag * jnp.sin(li * step)
    den = lr * lr + li * li
    nr = lbr - 1.0
    cr = (nr * lr + lbi * li) / den
    ci = (lbi * lr - nr * li) / den
    br, bi = br_ref[...], bi_ref[...]
    lbr_ref[...] = lbr
    lbi_ref[...] = lbi
    bbr_ref[...] = cr * br - ci * bi
    bbi_ref[...] = cr * bi + ci * br


def _s5_params(lp):
    n = 2 * S5_GROUPS * S5_STATE
    col = lambda a: a.reshape(n, 1)
    ls = jnp.broadcast_to(lp['s5_log_step'][:, :, None], (2, S5_GROUPS, S5_STATE))
    tr = 1024
    cspec = pl.BlockSpec((tr, 1), lambda i: (i, 0))
    bspec = pl.BlockSpec((tr, S5_GROUP_CH), lambda i: (i, 0))
    lbr, lbi, bbr, bbi = pl.pallas_call(
        _s5_param_kernel, grid=(n // tr,),
        in_specs=[cspec, cspec, cspec, bspec, bspec],
        out_specs=[cspec, cspec, bspec, bspec],
        out_shape=[jax.ShapeDtypeStruct((n, 1), F32)] * 2 + [jax.ShapeDtypeStruct((n, S5_GROUP_CH), F32)] * 2,
        compiler_params=_cparams(("parallel",)),
        name="s5_params",
    )(col(lp['s5_lambda_re']), col(lp['s5_lambda_im']), col(ls),
      lp['s5_B_re'].reshape(n, S5_GROUP_CH), lp['s5_B_im'].reshape(n, S5_GROUP_CH))
    gp = (2, S5_GROUPS, S5_STATE)
    return lbr.reshape(gp), lbi.reshape(gp), bbr.reshape(gp + (S5_GROUP_CH,)), bbi.reshape(gp + (S5_GROUP_CH,))


def _s5_block_weights(lbr, lbi, bbr, bbi, c_re, c_im):
    gl = LANES // S5_GROUP_CH
    nb = S5_GROUPS // gl
    eye = jnp.eye(gl, dtype=F32)

    def in_blk(bb):
        t = jnp.transpose(bb.reshape(2, nb, gl, S5_STATE, S5_GROUP_CH), (0, 1, 2, 4, 3))
        t = t[:, :, :, :, None, :] * eye[None, None, :, None, :, None]
        return t.reshape(2, nb, LANES, S5_BLK_STATES)

    def out_blk(c):
        t = jnp.transpose(c.reshape(2, nb, gl, S5_GROUP_CH, S5_STATE), (0, 1, 2, 4, 3))
        t = t[:, :, :, :, None, :] * eye[None, None, :, None, :, None]
        return t.reshape(2, nb, S5_BLK_STATES, LANES)

    w_in = jnp.concatenate([in_blk(bbr), in_blk(bbi)], axis=-1).astype(BF16)
    lam = jnp.stack([lbr.reshape(2, nb, S5_BLK_STATES), lbi.reshape(2, nb, S5_BLK_STATES)], axis=2)
    lam = jnp.transpose(lam, (1, 0, 2, 3)).reshape(nb, 4, S5_BLK_STATES)
    return w_in, lam, out_blk(c_re).astype(BF16), out_blk(c_im).astype(BF16)


def _s5_kernel(uf_ref, ub_ref, wf_ref, wb_ref, cfr_ref, cfi_ref, cbr_ref, cbi_ref, lam_ref, s0_ref,
               yf_ref, yb_ref, fin_ref, xf_scr, xb_scr, st_scr):
    tt, b, _ = uf_ref.shape
    ns = S5_BLK_STATES
    i = pl.program_id(1)

    @pl.when(i == 0)
    def _():
        st_scr[...] = s0_ref[0]

    xf_scr[...] = _dot(uf_ref[...].reshape(tt * b, LANES).astype(BF16), wf_ref[0, 0])
    xb_scr[...] = _dot(ub_ref[...].reshape(tt * b, LANES).astype(BF16), wb_ref[0, 0])
    lam = lam_ref[0]
    lfr, lfi, lbr, lbi = lam[0:1], lam[1:2], lam[2:3], lam[3:4]

    def step(t, carry):
        sfr, sfi, sbr, sbi = carry
        rf = pl.ds(pl.multiple_of(t * b, b), b)
        nfr = lfr * sfr - lfi * sfi + xf_scr[rf, :ns]
        nfi = lfr * sfi + lfi * sfr + xf_scr[rf, ns:]
        xf_scr[rf, :ns] = nfr
        xf_scr[rf, ns:] = nfi
        rb = pl.ds(pl.multiple_of((tt - 1 - t) * b, b), b)
        nbr = lbr * sbr - lbi * sbi + xb_scr[rb, :ns]
        nbi = lbr * sbi + lbi * sbr + xb_scr[rb, ns:]
        xb_scr[rb, :ns] = nbr
        xb_scr[rb, ns:] = nbi
        return nfr, nfi, nbr, nbi

    init = (st_scr[0, :, :ns], st_scr[0, :, ns:], st_scr[1, :, :ns], st_scr[1, :, ns:])
    sfr, sfi, sbr, sbi = lax.fori_loop(0, tt, step, init)
    st_scr[0, :, :ns] = sfr
    st_scr[0, :, ns:] = sfi
    st_scr[1, :, :ns] = sbr
    st_scr[1, :, ns:] = sbi

    yf = _dot(xf_scr[:, :ns].astype(BF16), cfr_ref[0, 0]) - _dot(xf_scr[:, ns:].astype(BF16), cfi_ref[0, 0])
    yb = _dot(xb_scr[:, :ns].astype(BF16), cbr_ref[0, 0]) - _dot(xb_scr[:, ns:].astype(BF16), cbi_ref[0, 0])
    yf_ref[...] = yf.reshape(tt, b, LANES)
    yb_ref[...] = yb.reshape(tt, b, LANES)

    @pl.when(i == pl.num_programs(1) - 1)
    def _():
        fin_ref[0] = st_scr[...]


def _s5(z3d, blk, s0, *, tt=64):
    L, B, _ = z3d.shape
    w_in, lam, c_re, c_im = blk
    nb = w_in.shape[1]
    tt = min(tt, L)
    nt = L // tt
    u0 = Z_S5_IN // LANES
    ns2 = 2 * S5_BLK_STATES
    wspec = lambda d: pl.BlockSpec((1, 1, LANES, ns2), lambda j, i: (d, j, 0, 0))
    cspec = lambda d: pl.BlockSpec((1, 1, S5_BLK_STATES, LANES), lambda j, i: (d, j, 0, 0))
    return pl.pallas_call(
        _s5_kernel, grid=(nb, nt),
        in_specs=[pl.BlockSpec((tt, B, LANES), lambda j, i: (i, 0, u0 + j)),
                  pl.BlockSpec((tt, B, LANES), lambda j, i: (nt - 1 - i, 0, u0 + j)),
                  wspec(0), wspec(1), cspec(0), cspec(0), cspec(1), cspec(1),
                  pl.BlockSpec((1, 4, S5_BLK_STATES), lambda j, i: (j, 0, 0)),
                  pl.BlockSpec((1, 2, B, ns2), lambda j, i: (j, 0, 0, 0))],
        out_specs=[pl.BlockSpec((tt, B, LANES), lambda j, i: (i, 0, j)),
                   pl.BlockSpec((tt, B, LANES), lambda j, i: (nt - 1 - i, 0, j)),
                   pl.BlockSpec((1, 2, B, ns2), lambda j, i: (j, 0, 0, 0))],
        out_shape=[jax.ShapeDtypeStruct((L, B, BRANCH_W), F32)] * 2
                  + [jax.ShapeDtypeStruct((nb, 2, B, ns2), F32)],
        scratch_shapes=[pltpu.VMEM((tt * B, ns2), F32), pltpu.VMEM((tt * B, ns2), F32),
                        pltpu.VMEM((2, B, ns2), F32)],
        compiler_params=_cparams(("parallel", "arbitrary")),
        name="s5_scan",
    )(z3d, z3d, w_in, w_in, c_re, c_im, c_re, c_im, lam, s0)


def _s5_state_to_blocks(s0):
    B = s0.shape[0]
    gl = LANES // S5_GROUP_CH
    nb = S5_GROUPS // gl
    t = s0.reshape(B, 2, nb, gl * S5_STATE, 2)
    t = jnp.transpose(t, (2, 1, 0, 4, 3))
    return t.reshape(nb, 2, B, 2 * S5_BLK_STATES)


def _s5_blocks_to_state(fin):
    nb, _, B, _ = fin.shape
    t = fin.reshape(nb, 2, B, 2, S5_BLK_STATES)
    t = jnp.transpose(t, (2, 1, 0, 4, 3))
    return t.reshape(B, 2, S5_GROUPS, S5_STATE, 2)


GATE_SLOTS = 8


def _gdn_gate_kernel(ab_ref, alog_ref, dtb_ref, col_ref, row_ref):
    L = ab_ref.shape[0]
    c = GDN_CHUNK
    slot = lax.broadcasted_iota(jnp.int32, (c, LANES), 1) % GATE_SLOTS
    r = lax.broadcasted_iota(jnp.int32, (c, c), 0)
    s = lax.broadcasted_iota(jnp.int32, (c, c), 1)
    lower = jnp.where(r >= s, 1.0, 0.0).astype(BF16)
    upper = jnp.where(r <= s, 1.0, 0.0).astype(BF16)
    neg_a = -jnp.exp(alog_ref[...])
    dtb = dtb_ref[...]

    def body(n, carry):
        rows = pl.ds(pl.multiple_of(n * c, c), c)
        x = ab_ref[rows, :]
        xa = x + dtb
        softplus = jnp.maximum(xa, 0.0) + jnp.log1p(jnp.exp(-jnp.abs(xa)))
        g = jnp.where((slot == 2) | (slot == 3), neg_a * softplus, 0.0)
        g1 = g.astype(BF16)
        r1 = g - g1.astype(F32)
        g2 = r1.astype(BF16)
        g3 = (r1 - g2.astype(F32)).astype(BF16)
        cf = _dot(lower, g1) + _dot(lower, g2) + _dot(lower, g3)
        cb = _dot(upper, g1) + _dot(upper, g2) + _dot(upper, g3)
        tile = jnp.where(slot < 2, jax.nn.sigmoid(x), jnp.where(slot == 2, cf, cb))
        col_ref[rows, :] = tile
        row_ref[0, pl.ds(n, 1)] = tile.T[None]
        return carry

    lax.fori_loop(0, L // c, body, 0)


def _gdn_gates(z2d, B, lp):
    L = z2d.shape[0]
    c = GDN_CHUNK
    ab0 = Z_AB // LANES
    zb = NZ // LANES

    def pad(a):
        t = jnp.zeros((GDN_HEADS, GATE_SLOTS), F32).at[:, 2:4].set(a.T)
        return jnp.pad(t.reshape(1, -1), ((0, 0), (0, LANES - GDN_HEADS * GATE_SLOTS)))

    return pl.pallas_call(
        _gdn_gate_kernel, grid=(B,),
        in_specs=[pl.BlockSpec((L, LANES), lambda b: (0, b * zb + ab0)),
                  pl.BlockSpec((1, LANES), lambda b: (0, 0)),
                  pl.BlockSpec((1, LANES), lambda b: (0, 0))],
        out_specs=[pl.BlockSpec((L, LANES), lambda b: (0, b)),
                   pl.BlockSpec((1, L // c, LANES, c), lambda b: (b, 0, 0, 0))],
        out_shape=[jax.ShapeDtypeStruct((L, B * LANES), F32),
                   jax.ShapeDtypeStruct((B, L // c, LANES, c), F32)],
        compiler_params=_cparams(("parallel",)),
        name="gdn_gates",
    )(z2d, pad(lp['gdn_A_log']), pad(lp['gdn_dt_bias']))


def _l2norm(x):
    return x * lax.rsqrt(jnp.sum(x * x, axis=-1, keepdims=True) + NORM_EPS)


def _chunk_masks():
    c = GDN_CHUNK
    r = lax.broadcasted_iota(jnp.int32, (c, c), 0)
    s = lax.broadcasted_iota(jnp.int32, (c, c), 1)
    return ((r >= s, r > s), (r <= s, r < s))


def _head_gates(col_ref, rows, h):
    tile = col_ref[rows, :]
    return pltpu.roll(tile, LANES - GATE_SLOTS * h, axis=1)[:, :GATE_SLOTS]


def _chunk_decay(gates, row_ref, ci, h, d, incl):
    beta = gates[:, d:d + 1]
    gcum = gates[:, 2 + d:3 + d]
    grow = row_ref[0, ci, pl.ds(GATE_SLOTS * h + 2 + d, 1), :]
    decay = jnp.where(incl, jnp.exp(jnp.where(incl, gcum - grow, 0.0)), 0.0)
    return beta, gcum, decay


def _dwconv3_rows(x_ref, w, r0, n):
    L = x_ref.shape[0]
    x = x_ref[pl.ds(r0, n), :]
    before = x_ref[pl.ds(jnp.maximum(r0 - 1, 0), 1), :]
    after = x_ref[pl.ds(jnp.minimum(r0 + n, L - 1), 1), :]
    before = jnp.where(r0 == 0, 0.0, before)
    after = jnp.where(r0 + n == L, 0.0, after)
    r = lax.broadcasted_iota(jnp.int32, x.shape, 0)
    prev = jnp.where(r == 0, before, pltpu.roll(x, 1, axis=0))
    nxt = jnp.where(r == n - 1, after, pltpu.roll(x, n - 1, axis=0))
    return w[0:1] * prev + w[1:2] * x + w[2:3] * nxt


def _gdn_amat_kernel(k_ref, wk_ref, col_ref, row_ref, a_ref):
    c = GDN_CHUNK
    nc = k_ref.shape[0] // c
    h = pl.program_id(1)
    wk = wk_ref[...]
    masks = _chunk_masks()

    def body(n, carry):
        r0 = pl.multiple_of(n * c, c)
        kc = _l2norm(_silu(_dwconv3_rows(k_ref, wk, r0, c)))
        gates = _head_gates(col_ref, pl.ds(r0, c), h)
        for d in range(2):
            incl, strict = masks[d]
            beta, _, decay = _chunk_decay(gates, row_ref, n, h, d, incl)
            kk = lax.dot_general((kc * beta).astype(BF16), kc.astype(BF16),
                                 (((1,), (1,)), ((), ())), preferred_element_type=F32)
            a_ref[d, 0, 0, pl.ds(n, 1)] = jnp.where(strict, kk * decay, 0.0)[None]
        return carry

    lax.fori_loop(0, nc, body, 0)


def _tri_inverse_kernel(a_ref, t_ref, *, upper):
    n, _, pw = a_ref.shape
    col = lax.broadcasted_iota(jnp.int32, (n, pw), 0)
    for i in (range(n - 1, -1, -1) if upper else range(n)):
        lo, hi = (i + 1, n) if upper else (0, i)

        def inner(j, acc, i=i):
            return acc - a_ref[i, pl.ds(j, 1), :] * t_ref[j]

        t_ref[i] = lax.fori_loop(lo, hi, inner, jnp.where(col == i, 1.0, 0.0), unroll=2)


def _tri_inverse(a, *, upper):
    c, _, P = a.shape
    pw = 2 * LANES if P % (2 * LANES) == 0 else LANES
    pp = -(-P // pw) * pw
    if pp != P:
        a = jnp.pad(a, ((0, 0), (0, 0), (0, pp - P)))
    spec = pl.BlockSpec((c, c, pw), lambda p: (0, 0, p))
    t = pl.pallas_call(
        functools.partial(_tri_inverse_kernel, upper=upper), grid=(pp // pw,),
        in_specs=[spec], out_specs=spec, out_shape=jax.ShapeDtypeStruct((c, c, pp), F32),
        compiler_params=_cparams(("parallel",)),
        name="gdn_tri_inverse",
    )(a)
    return t[:, :, :P]


def _gdn_kernel(q_ref, k_ref, v_ref, wq_ref, wk_ref, wv_ref, col_ref, row_ref, t_ref, ng_ref, s0_ref,
                o_ref, sfin_ref, mo_scr, n_scr, ou_scr, st_scr):
    L = q_ref.shape[0]
    c = GDN_CHUNK
    nc = L // c
    h = pl.program_id(1)
    wq, wk, wv = wq_ref[...], wk_ref[...], wv_ref[...]
    st_scr[...] = s0_ref[0, :, 0]
    masks = _chunk_masks()

    def g_last_of(gcum, d):
        return gcum[c - 1:c] if d == 0 else gcum[0:1]

    def prepare(ci, d, gates, qc, kc, vc):
        rows = pl.ds(pl.multiple_of(ci * c, c), c)
        beta, gcum, decay = _chunk_decay(gates, row_ref, ci, h, d, masks[d][0])
        qk = decay * lax.dot_general(qc.astype(BF16), kc.astype(BF16),
                                     (((1,), (1,)), ((), ())), preferred_element_type=F32)
        t_mat = t_ref[d, 0, 0, pl.ds(ci, 1)][0]
        egc = jnp.exp(gcum)
        uw = _bdot(t_mat, jnp.concatenate([vc * beta, kc * (beta * egc)], axis=1))
        kdec = kc * jnp.exp(g_last_of(gcum, d) - gcum)
        lhs = jnp.concatenate([kdec.T, qk], axis=0)
        res = _bdot(lhs, uw)
        n_scr[d, pl.ds(ci, 1)] = res[:GDN_DK, :GDN_DV][None]
        mo = jnp.concatenate([-res[:GDN_DK, GDN_DV:], qc * egc - res[GDN_DK:, GDN_DV:]], axis=0)
        mo_scr[d, pl.ds(ci, 1)] = mo.astype(BF16)[None]
        ou_scr[d, rows, :] = res[GDN_DK:, :GDN_DV]

    def phase1(n, carry):
        r0 = pl.multiple_of(n * c, c)
        qc = _l2norm(_silu(_dwconv3_rows(q_ref, wq, r0, c))) * (GDN_DK ** -0.5)
        kc = _l2norm(_silu(_dwconv3_rows(k_ref, wk, r0, c)))
        vc = _silu(_dwconv3_rows(v_ref, wv, r0, c))
        gates = _head_gates(col_ref, pl.ds(r0, c), h)
        prepare(n, 0, gates, qc, kc, vc)
        prepare(n, 1, gates, qc, kc, vc)
        return carry

    lax.fori_loop(0, nc, phase1, 0, unroll=2)

    def advance(ci, d):
        rows = pl.ds(pl.multiple_of(ci * c, c), c)
        gcum = _head_gates(col_ref, rows, h)[:, 2 + d:3 + d]
        state = st_scr[d]
        r = _dot(mo_scr[d, pl.ds(ci, 1)][0], state.astype(BF16))
        st_scr[d] = state * jnp.exp(g_last_of(gcum, d)) + r[:GDN_DK] + n_scr[d, pl.ds(ci, 1)][0]
        return rows, r[GDN_DK:]

    def phase2(n, carry):
        rows, o = advance(n, 0)
        o_ref[rows, :] = o + ou_scr[0, rows, :]
        rows, o = advance(nc - 1 - n, 1)
        ou_scr[1, rows, :] = o + ou_scr[1, rows, :]
        return carry

    lax.fori_loop(0, nc, phase2, 0)
    o = o_ref[...] + ou_scr[1]
    o_ref[...] = o * lax.rsqrt(jnp.mean(o * o, axis=-1, keepdims=True) + NORM_EPS) * ng_ref[...]
    sfin_ref[0, :, 0] = st_scr[...]


def _gdn(z2d, B, lp, s0):
    L = z2d.shape[0]
    H = GDN_HEADS
    c = GDN_CHUNK
    nc = L // c
    cols, rows = _gdn_gates(z2d, B, lp)
    zb = NZ // LANES
    q0 = Z_GDN_IN // LANES
    zspec = lambda part: pl.BlockSpec((L, LANES), lambda b, h: (0, b * zb + q0 + part * H + h))
    wspec = lambda part: pl.BlockSpec((3, LANES), lambda b, h: (0, part * H + h))
    colspec = pl.BlockSpec((L, LANES), lambda b, h: (0, b))
    rowspec = pl.BlockSpec((1, nc, LANES, c), lambda b, h: (b, 0, 0, 0))
    tspec = pl.BlockSpec((2, 1, 1, nc, c, c), lambda b, h: (0, b, h, 0, 0, 0))
    sspec = pl.BlockSpec((1, 2, 1, GDN_DK, GDN_DV), lambda b, h: (b, 0, h, 0, 0))
    cw = lp['gdn_conv_w']
    cp = _cparams(("parallel", "parallel"))
    a_mat = pl.pallas_call(
        _gdn_amat_kernel, grid=(B, H),
        in_specs=[zspec(1), wspec(1), colspec, rowspec],
        out_specs=tspec, out_shape=jax.ShapeDtypeStruct((2, B, H, nc, c, c), F32),
        compiler_params=cp, name="gdn_amat",
    )(z2d, cw, cols, rows)
    P = B * H * nc
    a_t = jnp.swapaxes(a_mat.reshape(2, P, c * c), 1, 2).reshape(2, c, c, P)
    t_t = jnp.stack([_tri_inverse(a_t[0], upper=False), _tri_inverse(a_t[1], upper=True)])
    t_mat = jnp.swapaxes(t_t.reshape(2, c * c, P), 1, 2).reshape(2, B, H, nc, c, c)
    return pl.pallas_call(
        _gdn_kernel, grid=(B, H),
        in_specs=[zspec(0), zspec(1), zspec(2), wspec(0), wspec(1), wspec(2), colspec, rowspec, tspec,
                  pl.BlockSpec((1, LANES), lambda b, h: (0, 0)), sspec],
        out_specs=[pl.BlockSpec((L, LANES), lambda b, h: (0, b * H + h)), sspec],
        out_shape=[jax.ShapeDtypeStruct((L, B * BRANCH_W), F32),
                   jax.ShapeDtypeStruct((B, 2, H, GDN_DK, GDN_DV), F32)],
        scratch_shapes=[pltpu.VMEM((2, nc, GDN_DK + c, GDN_DV), BF16),
                          pltpu.VMEM((2, nc, GDN_DK, GDN_DV), F32),
                          pltpu.VMEM((2, L, GDN_DV), F32),
                          pltpu.VMEM((2, GDN_DK, GDN_DV), F32)],
        compiler_params=cp, name="gdn",
    )(z2d, z2d, z2d, cw, cw, cw, cols, rows, t_mat, lp['gdn_norm_g'].reshape(1, LANES), s0)


def _merge_kernel(x_ref, hy_ref, yf_ref, yb_ref, og_ref, hyg_ref, u_ref, s5g_ref, gdg_ref,
                  m0_ref, m1_ref, m2_ref, d_ref, gw_ref, gb_ref, wb_ref, wo_ref, gate_ref, fg_ref,
                  out_ref, *, final_norm):
    tl, b, d = x_ref.shape
    rows = tl * b
    flat = lambda ref: ref[...].reshape(rows, ref.shape[-1])
    y_hy = flat(hy_ref) * _silu(flat(hyg_ref))
    s5_raw = d_ref[...] * flat(u_ref) + flat(yf_ref) + flat(yb_ref)
    gelu = 0.5 * s5_raw * (1.0 + jnp.tanh(math.sqrt(2.0 / math.pi) * (s5_raw + 0.044715 * (s5_raw * s5_raw * s5_raw))))
    glu = _dot(gelu.astype(BF16), gw_ref[...]) + gb_ref[...]
    y_s5 = glu[:, :BRANCH_W] * jax.nn.sigmoid(glu[:, BRANCH_W:]) * _silu(flat(s5g_ref))
    y_gdn = flat(og_ref) * _silu(flat(gdg_ref))
    merged = jax.nn.sigmoid(flat(m0_ref)) * _dot(y_hy.astype(BF16), wb_ref[0])
    merged += jax.nn.sigmoid(flat(m1_ref)) * _dot(y_s5.astype(BF16), wb_ref[1])
    merged += jax.nn.sigmoid(flat(m2_ref)) * _dot(y_gdn.astype(BF16), wb_ref[2])
    delta = _dot(merged.astype(BF16), wo_ref[...]).reshape(tl, b, d)
    x = x_ref[...] + gate_ref[...] * delta
    if final_norm:
        x = x * lax.rsqrt(jnp.mean(x * x, axis=-1, keepdims=True) + NORM_EPS) * fg_ref[...]
    out_ref[...] = x


def _merge(x, hy, yf, yb, og, z3d, gate, lp, final_g, *, rows_per_tile=256):
    L, B, _ = x.shape
    tl = max(1, min(L, rows_per_tile // B))
    act = pl.BlockSpec((tl, B, BRANCH_W), lambda i: (i, 0, 0))
    zspec = lambda col: pl.BlockSpec((tl, B, BRANCH_W), lambda i: (i, 0, col // BRANCH_W))
    once = pl.Buffered(1)
    const = lambda *shape: pl.BlockSpec(shape, lambda i: (0,) * len(shape), pipeline_mode=once)
    return pl.pallas_call(
        functools.partial(_merge_kernel, final_norm=final_g is not None),
        grid=(L // tl,),
        in_specs=[act, act, act, act, act,
                  zspec(Z_HY_GATE), zspec(Z_S5_IN), zspec(Z_S5_GATE), zspec(Z_GDN_GATE),
                  zspec(Z_MERGE), zspec(Z_MERGE + D_MODEL), zspec(Z_MERGE + 2 * D_MODEL),
                  const(1, BRANCH_W), const(BRANCH_W, 2 * BRANCH_W), const(1, 2 * BRANCH_W),
                  const(3, BRANCH_W, D_MODEL), const(D_MODEL, D_MODEL),
                  const(1, B, D_MODEL), const(1, 1, D_MODEL)],
        out_specs=act,
        out_shape=jax.ShapeDtypeStruct((L, B, D_MODEL), F32),
        compiler_params=_cparams(("parallel",)),
        name="merge",
    )(x, hy, yf, yb, og, z3d, z3d, z3d, z3d, z3d, z3d, z3d,
      lp['s5_D'].reshape(1, BRANCH_W), lp['s5_glu_w'].astype(BF16), lp['s5_glu_b'].reshape(1, -1),
      lp['w_branch'].astype(BF16), lp['w_out'].astype(BF16), gate[None],
      (final_g if final_g is not None else jnp.ones((D_MODEL,), F32)).reshape(1, 1, D_MODEL))


def _pack_w_in(w_in):
    ab = w_in[:, W_IN_AB:W_IN_AB + 4 * GDN_HEADS].reshape(D_MODEL, 4, GDN_HEADS)
    ab = jnp.pad(jnp.transpose(ab, (0, 2, 1)), ((0, 0), (0, 0), (0, GATE_SLOTS - 4)))
    ab = ab.reshape(D_MODEL, GDN_HEADS * GATE_SLOTS)
    pad = jnp.zeros((D_MODEL, NZ - Z_AB - GDN_HEADS * GATE_SLOTS), w_in.dtype)
    return jnp.concatenate([w_in[:, :W_IN_AB], w_in[:, W_IN_AB + 4 * GDN_HEADS:], ab, pad], axis=1).astype(BF16)


def _trunk_layer(x, pos, mod, lp, shared, s0_gdn, s0_s5, final_g):
    L, B, _ = x.shape
    mod3 = jnp.transpose(mod.reshape(B, 3, D_MODEL), (1, 0, 2))
    z3d, x = _in_projection(x, pos, mod3, lp['norm_g'], shared['w_packed'])
    z2d = z3d.reshape(L, B * NZ)
    tables = shared['tables'][L]
    spectra = _hyena_filter_spectra(L, lp, (tables['zpos'], tables['decay'], tables['dft'][0], tables['dft'][1]))
    hy = _hyena(z2d, B, lp, spectra, tables['dft'])
    yf, yb, s5_fin = _s5(z3d, shared['s5_blk'], _s5_state_to_blocks(s0_s5))
    og, gdn_fin = _gdn(z2d, B, lp, s0_gdn)
    x = _merge(x, hy.reshape(L, B, BRANCH_W), yf, yb, og.reshape(L, B, BRANCH_W), z3d, mod3[2], lp, final_g)
    return x, gdn_fin, _s5_blocks_to_state(s5_fin)


def _grid_pos_embed(n_tokens, dim):
    rows = n_tokens // GRID_W
    t = jnp.arange(rows * GRID_W)
    r = (t // GRID_W).astype(F32)
    col = (t % GRID_W).astype(F32)
    quarter = dim // 4
    omega = 1.0 / (10000.0 ** (jnp.arange(quarter, dtype=F32) / quarter))
    er = r[:, None] * omega
    ec = col[:, None] * omega
    return jnp.concatenate([jnp.sin(er), jnp.cos(er), jnp.sin(ec), jnp.cos(ec)], axis=-1)


_LAYER_KEYS = ('norm_g', 'w_in', 'hy_conv_w', 'hy_conv_b', 'hy_f_w1', 'hy_f_b1', 'hy_f_w2', 'hy_f_b2',
               'hy_f_w3', 'hy_f_freq', 'hy_bias', 's5_lambda_re', 's5_lambda_im', 's5_log_step',
               's5_B_re', 's5_B_im', 's5_C_re', 's5_C_im', 's5_D', 's5_glu_w', 's5_glu_b', 'gdn_conv_w',
               'gdn_A_log', 'gdn_dt_bias', 'gdn_norm_g', 'w_branch', 'w_out')


def kernel(x_prompt, x_sample, c, c_ctx, state_gdn, state_s5, w_mod, b_mod, norm_g, w_in, hy_conv_w, hy_conv_b, hy_f_w1, hy_f_b1, hy_f_w2, hy_f_b2, hy_f_w3, hy_f_freq, hy_bias, s5_lambda_re, s5_lambda_im, s5_log_step, s5_B_re, s5_B_im, s5_C_re, s5_C_im, s5_D, s5_glu_w, s5_glu_b, gdn_conv_w, gdn_A_log, gdn_dt_bias, gdn_norm_g, w_branch, w_out, final_norm_g):
    params = dict(norm_g=norm_g, w_in=w_in, hy_conv_w=hy_conv_w, hy_conv_b=hy_conv_b, hy_f_w1=hy_f_w1,
                  hy_f_b1=hy_f_b1, hy_f_w2=hy_f_w2, hy_f_b2=hy_f_b2, hy_f_w3=hy_f_w3, hy_f_freq=hy_f_freq,
                  hy_bias=hy_bias, s5_lambda_re=s5_lambda_re, s5_lambda_im=s5_lambda_im,
                  s5_log_step=s5_log_step, s5_B_re=s5_B_re, s5_B_im=s5_B_im, s5_C_re=s5_C_re,
                  s5_C_im=s5_C_im, s5_D=s5_D, s5_glu_w=s5_glu_w, s5_glu_b=s5_glu_b, gdn_conv_w=gdn_conv_w,
                  gdn_A_log=gdn_A_log, gdn_dt_bias=gdn_dt_bias, gdn_norm_g=gdn_norm_g, w_branch=w_branch,
                  w_out=w_out)
    depth = w_in.shape[0]
    bp, lc, _ = x_prompt.shape
    bl, ll, _ = x_sample.shape

    cc = jnp.concatenate([c_ctx[None], c], axis=0)
    n_rows = -(-cc.shape[0] // 8) * 8
    cc = jnp.pad(cc, ((0, n_rows - cc.shape[0]), (0, 0)))
    mods = _modulation(cc, w_mod, b_mod)

    tables = {}
    for L in {lc, ll}:
        zpos, decay = _filter_tables(L)
        tables[L] = dict(zpos=zpos, decay=decay, dft=_dft_tables(L))

    xc = jnp.transpose(x_prompt, (1, 0, 2))
    xl = jnp.transpose(x_sample, (1, 0, 2))
    pos = _grid_pos_embed(ll, D_MODEL).astype(x_sample.dtype).reshape(ll, 1, D_MODEL)
    zero_gdn = jnp.zeros((bp, 2, GDN_HEADS, GDN_DK, GDN_DV), F32)
    zero_s5 = jnp.zeros((bp, 2, S5_GROUPS, S5_STATE, 2), F32)
    gdn_states, s5_states = [], []
    for l in range(depth):
        lp = {k: params[k][l] for k in _LAYER_KEYS}
        lbr, lbi, bbr, bbi = _s5_params(lp)
        shared = dict(w_packed=_pack_w_in(lp['w_in']), tables=tables,
                      s5_blk=_s5_block_weights(lbr, lbi, bbr, bbi, lp['s5_C_re'], lp['s5_C_im']))
        fg = final_norm_g if l == depth - 1 else None
        mod_ctx = jnp.broadcast_to(mods[l, 0:1], (bp, 3 * D_MODEL))
        mod_lat = mods[l, 1:1 + bl]
        xc, fin_gdn, fin_s5 = _trunk_layer(xc, None, mod_ctx, lp, shared, zero_gdn, zero_s5, fg)
        gdn_states.append(fin_gdn)
        s5_states.append(fin_s5)
        xl, _, _ = _trunk_layer(xl, pos if l == 0 else None, mod_lat, lp, shared,
                                state_gdn[:, l].astype(F32), state_s5[:, l].astype(F32), fg)
    y_prompt = jnp.transpose(xc, (1, 0, 2))
    y_sample = jnp.transpose(xl, (1, 0, 2))
    return (y_prompt, y_sample, jnp.stack(gdn_states, axis=1), jnp.stack(s5_states, axis=1))
```

```python
import functools
import math

import jax
import jax.numpy as jnp
from jax import lax
from jax.experimental import pallas as pl
from jax.experimental.pallas import tpu as pltpu

F32 = jnp.float32
BF16 = jnp.bfloat16

D_MODEL = 1024
BRANCH_W = 1024
GRID_W = 64
HY_ORDER = 2
HY_EMB = 33
HY_HID = 64
HY_DECAY_TARGET = 1e-2
HY_SHORT_DECAY_PCT = 0.3
HY_LONG_DECAY_PCT = 1.5
S5_GROUP_CH = 16
S5_GROUPS = 64
S5_STATE = 64
GDN_DK = 128
GDN_DV = 128
GDN_HEADS = 8
GDN_CHUNK = 64
NORM_EPS = 1e-6

LANES = 128
Z_HY_IN, Z_HY_GATE, Z_S5_IN, Z_S5_GATE = 0, 3072, 4096, 5120
Z_GDN_IN, Z_GDN_GATE, Z_MERGE, Z_AB = 6144, 9216, 10240, 13312
NZ = 14336
W_IN_AB = 9216
VMEM_LIMIT_V7X = 56 * 1024 * 1024


def _cparams(sem):
    return pltpu.CompilerParams(dimension_semantics=sem, vmem_limit_bytes=VMEM_LIMIT_V7X)


def _dot(a, b):
    return jnp.dot(a, b, preferred_element_type=F32)


def _bdot(a, b):
    return jnp.dot(a.astype(BF16), b.astype(BF16), preferred_element_type=F32)


def _split(a):
    hi = a.astype(BF16)
    lo = (a - hi.astype(F32)).astype(BF16)
    return hi, lo


def _dot3(a, b):
    ah, al = _split(a)
    bh, bl = _split(b)
    return _dot(ah, bh) + _dot(al, bh) + _dot(ah, bl)


def _silu(x):
    return x * jax.nn.sigmoid(x)


def _shift_prev(x):
    r = lax.broadcasted_iota(jnp.int32, x.shape, 0)
    return jnp.where(r == 0, 0.0, pltpu.roll(x, 1, axis=0))


def _shift_next(x):
    n = x.shape[0]
    r = lax.broadcasted_iota(jnp.int32, x.shape, 0)
    return jnp.where(r == n - 1, 0.0, pltpu.roll(x, n - 1, axis=0))


def _dwconv3(x, w):
    return w[0:1] * _shift_prev(x) + w[1:2] * x + w[2:3] * _shift_next(x)


def _mod_kernel(c_ref, w_ref, b_ref, o_ref):
    o_ref[0] = _dot3(_silu(c_ref[...]), w_ref[0]) + b_ref[0]


def _modulation(cc, w_mod, b_mod):
    depth = w_mod.shape[0]
    r = cc.shape[0]
    return pl.pallas_call(
        _mod_kernel,
        grid=(depth, 3),
        in_specs=[pl.BlockSpec((r, D_MODEL), lambda l, n: (0, 0)),
                  pl.BlockSpec((1, D_MODEL, D_MODEL), lambda l, n: (l, 0, n)),
                  pl.BlockSpec((1, 1, D_MODEL), lambda l, n: (l, 0, n))],
        out_specs=pl.BlockSpec((1, r, D_MODEL), lambda l, n: (l, 0, n)),
        out_shape=jax.ShapeDtypeStruct((depth, r, 3 * D_MODEL), F32),
        compiler_params=_cparams(("arbitrary", "arbitrary")),
        name="modulation",
    )(cc, w_mod, b_mod.reshape(depth, 1, 3 * D_MODEL))


def _inproj_kernel(x_ref, mod_ref, g_ref, w_ref, z_ref, h_scr):
    tl, b, d = x_ref.shape

    @pl.when(pl.program_id(1) == 0)
    def _():
        x = x_ref[...]
        y = x * lax.rsqrt(jnp.mean(x * x, axis=-1, keepdims=True) + NORM_EPS) * g_ref[...]
        h = y * (1.0 + mod_ref[1]) + mod_ref[0]
        h_scr[...] = h.reshape(tl * b, d).astype(BF16)

    z = _dot(h_scr[...], w_ref[...])
    z_ref[...] = z.reshape(tl, b, z.shape[-1])


def _inproj_entry_kernel(*refs, has_pos):
    if has_pos:
        x_ref, pos_ref, mod_ref, g_ref, w_ref, z_ref, xres_ref, h_scr, x_il, h_il = refs
    else:
        x_ref, mod_ref, g_ref, w_ref, z_ref, xres_ref, h_scr, x_il, h_il = refs
    b, tl, d = x_ref.shape

    @pl.when(pl.program_id(1) == 0)
    def _():
        for i in range(b):
            x = x_ref[i]
            if has_pos:
                x = x + pos_ref[...]
            y = x * lax.rsqrt(jnp.mean(x * x, axis=-1, keepdims=True) + NORM_EPS) * g_ref[...]
            h = y * (1.0 + mod_ref[1, i:i + 1, :]) + mod_ref[0, i:i + 1, :]
            rows = pl.ds(i, tl, stride=b)
            for k in range(d // LANES):
                x_il[k, rows, :] = x[:, k * LANES:(k + 1) * LANES]
                h_il[k, rows, :] = h[:, k * LANES:(k + 1) * LANES]
        xres_ref[...] = jnp.concatenate([x_il[k] for k in range(d // LANES)], axis=1)
        h_scr[...] = jnp.concatenate([h_il[k] for k in range(d // LANES)], axis=1).astype(BF16)

    z = _dot(h_scr[...], w_ref[...])
    z_ref[...] = z.reshape(tl, b, z.shape[-1])


def _in_projection_entry(x, pos, mod3, norm_g, w_packed, *, rows_per_tile=512, tn=2048):
    B, L, _ = x.shape
    tl = min(L, rows_per_tile // B)
    has_pos = pos is not None
    in_specs = [pl.BlockSpec((B, tl, D_MODEL), lambda i, j: (0, i, 0))]
    args = [x]
    if has_pos:
        in_specs.append(pl.BlockSpec((tl, D_MODEL), lambda i, j: (i, 0)))
        args.append(pos)
    in_specs += [pl.BlockSpec((3, B, D_MODEL), lambda i, j: (0, 0, 0)),
                 pl.BlockSpec((1, D_MODEL), lambda i, j: (0, 0)),
                 pl.BlockSpec((D_MODEL, tn), lambda i, j: (0, j))]
    args += [mod3, norm_g.reshape(1, D_MODEL), w_packed]
    z, xres = pl.pallas_call(
        functools.partial(_inproj_entry_kernel, has_pos=has_pos),
        grid=(L // tl, NZ // tn), in_specs=in_specs,
        out_specs=[pl.BlockSpec((tl, B, tn), lambda i, j: (i, 0, j)),
                   pl.BlockSpec((tl * B, D_MODEL), lambda i, j: (i, 0))],
        out_shape=[jax.ShapeDtypeStruct((L, B, NZ), F32), jax.ShapeDtypeStruct((L * B, D_MODEL), F32)],
        scratch_shapes=[pltpu.VMEM((tl * B, D_MODEL), BF16),
                        pltpu.VMEM((D_MODEL // LANES, tl * B, LANES), F32),
                        pltpu.VMEM((D_MODEL // LANES, tl * B, LANES), F32)],
        compiler_params=_cparams(("parallel", "arbitrary")),
        name="in_projection_entry",
    )(*args)
    return z, xres.reshape(L, B, D_MODEL)


def _in_projection(x, mod3, norm_g, w_packed, *, rows_per_tile=1024, tn=2048):
    L, B, _ = x.shape
    tl = min(L, rows_per_tile // B)
    return pl.pallas_call(
        _inproj_kernel,
        grid=(L // tl, NZ // tn),
        in_specs=[pl.BlockSpec((tl, B, D_MODEL), lambda i, j: (i, 0, 0)),
                  pl.BlockSpec((3, B, D_MODEL), lambda i, j: (0, 0, 0)),
                  pl.BlockSpec((1, 1, D_MODEL), lambda i, j: (0, 0, 0)),
                  pl.BlockSpec((D_MODEL, tn), lambda i, j: (0, j))],
        out_specs=pl.BlockSpec((tl, B, tn), lambda i, j: (i, 0, j)),
        out_shape=jax.ShapeDtypeStruct((L, B, NZ), F32),
        scratch_shapes=[pltpu.VMEM((tl * B, D_MODEL), BF16)],
        compiler_params=_cparams(("parallel", "arbitrary")),
        name="in_projection",
    )(x, mod3, norm_g.reshape(1, 1, D_MODEL), w_packed)


def _dft_tables(L):
    f = lax.broadcasted_iota(jnp.int32, (L, L), 0)
    t = lax.broadcasted_iota(jnp.int32, (L, L), 1)
    ang = ((f * t) % (2 * L)).astype(F32) * (math.pi / L)
    c = jnp.cos(ang).astype(BF16)
    s = jnp.sin(ang)
    fwd_s = jnp.where(f == 0, jnp.where(t % 2 == 0, 1.0, -1.0), s).astype(BF16)
    inv_s = jnp.where(t == 0, jnp.where(f % 2 == 0, 1.0, -1.0), s).astype(BF16)
    return c, fwd_s, c, inv_s


def _filter_tables(L):
    t = jnp.linspace(0.0, 1.0, L, dtype=F32)[:, None]
    bands = (HY_EMB - 1) // 2
    fb = jnp.linspace(1e-4, bands - 1, bands, dtype=F32)[None, :]
    wpos = 2.0 * math.pi * jnp.arange(L, dtype=F32)[:, None] / L
    zpos = jnp.concatenate([t, jnp.cos(fb * wpos), -jnp.sin(fb * wpos)], axis=-1)
    deltas = jnp.abs(jnp.linspace(math.log(HY_DECAY_TARGET) / HY_LONG_DECAY_PCT,
                                  math.log(HY_DECAY_TARGET) / HY_SHORT_DECAY_PCT, BRANCH_W, dtype=F32))
    decay = jnp.exp(-t * deltas)
    return zpos, decay


def _hy_filter_kernel(zpos_ref, w1_ref, b1_ref, w2_ref, b2_ref, freq_ref, w3_ref, decay_ref,
                      fc_ref, fs_ref, p_ref, q_ref, kl_ref):
    freq = freq_ref[...]
    hdn = jnp.sin(freq * (_dot3(zpos_ref[...], w1_ref[...]) + b1_ref[...]))
    hdn = jnp.sin(freq * (_dot3(hdn, w2_ref[...]) + b2_ref[...]))
    decay = decay_ref[...]
    L = decay.shape[0]
    row = lax.broadcasted_iota(jnp.int32, decay.shape, 0)
    alt = jnp.where(row % 2 == 0, 1.0, -1.0)
    bin_w = jnp.where(row == 0, 1.0, 2.0) / (2.0 * L)
    for o in range(HY_ORDER):
        fwd = _dot3(hdn, w3_ref[2 * o]) * decay
        bwd = jnp.where(row == 0, 0.0, _dot3(hdn, w3_ref[2 * o + 1]) * decay)
        norm = jnp.sum(jnp.abs(fwd) + jnp.abs(bwd), axis=0, keepdims=True)
        a = (fwd + bwd) / norm
        s = (fwd - bwd) / norm
        p_ref[o] = _bdot(fc_ref[...], a) * bin_w
        q_ref[o] = jnp.where(row == 0, 0.0, _bdot(fs_ref[...], s) * bin_w)
        kl_ref[o] = jnp.sum(alt * a, axis=0, keepdims=True) / (2.0 * L)


def _hyena_filter_spectra(L, lp, tables, *, cb=256):
    zpos, decay, fc, fs = tables
    w3 = jnp.transpose(lp['hy_f_w3'].reshape(HY_HID, HY_ORDER * 2, BRANCH_W), (1, 0, 2))
    const = lambda *shape: pl.BlockSpec(shape, lambda j: (0,) * len(shape))
    return pl.pallas_call(
        _hy_filter_kernel,
        grid=(BRANCH_W // cb,),
        in_specs=[const(L, HY_EMB), const(HY_EMB, HY_HID), const(1, HY_HID), const(HY_HID, HY_HID),
                  const(1, HY_HID), const(1, HY_HID),
                  pl.BlockSpec((HY_ORDER * 2, HY_HID, cb), lambda j: (0, 0, j)),
                  pl.BlockSpec((L, cb), lambda j: (0, j)),
                  const(L, L), const(L, L)],
        out_specs=[pl.BlockSpec((HY_ORDER, L, cb), lambda j: (0, 0, j)),
                   pl.BlockSpec((HY_ORDER, L, cb), lambda j: (0, 0, j)),
                   pl.BlockSpec((HY_ORDER, 1, cb), lambda j: (0, 0, j))],
        out_shape=[jax.ShapeDtypeStruct((HY_ORDER, L, BRANCH_W), F32),
                   jax.ShapeDtypeStruct((HY_ORDER, L, BRANCH_W), F32),
                   jax.ShapeDtypeStruct((HY_ORDER, 1, BRANCH_W), F32)],
        compiler_params=_cparams(("arbitrary",)),
        name="hyena_filter",
    )(zpos, lp['hy_f_w1'], lp['hy_f_b1'].reshape(1, HY_HID), lp['hy_f_w2'], lp['hy_f_b2'].reshape(1, HY_HID),
      lp['hy_f_freq'].reshape(1, HY_HID), w3, decay, fc, fs)


def _hy_fwd_kernel(*refs, first):
    if first:
        zin_ref, cw_ref, cbias_ref, p_ref, q_ref, kl_ref, fc_ref, fs_ref, yr_ref, yn_ref = refs
        zin = _dwconv3(zin_ref[...], cw_ref[...]) + cbias_ref[...]
    else:
        zin_ref, p_ref, q_ref, kl_ref, fc_ref, fs_ref, yr_ref, yn_ref = refs
        zin = zin_ref[...]
    zb = zin.astype(BF16)
    zr = _dot(fc_ref[...], zb)
    w = _dot(fs_ref[...], zb)
    p = p_ref[0]
    q = q_ref[0]
    row = lax.broadcasted_iota(jnp.int32, p.shape, 0)
    pd = jnp.where(row == 0, kl_ref[0], p)
    yr_ref[...] = (zr * p - w * q).astype(BF16)
    yn_ref[...] = (zr * q + w * pd).astype(BF16)


def _hy_inv_kernel(*refs, first):
    if first:
        (yr_ref, yn_ref, gc_ref, gs_ref, zin_ref, cwz_ref, cbz_ref, xg_ref, cwx_ref, cbx_ref,
         bias_ref, out_ref) = refs
        zin = _dwconv3(zin_ref[...], cwz_ref[...]) + cbz_ref[...]
    else:
        yr_ref, yn_ref, gc_ref, gs_ref, zin_ref, xg_ref, cwx_ref, cbx_ref, bias_ref, out_ref = refs
        zin = zin_ref[...]
    conv = _dot(gc_ref[...], yr_ref[...]) + _dot(gs_ref[...], yn_ref[...])
    xg = _dwconv3(xg_ref[...], cwx_ref[...]) + cbx_ref[...]
    out_ref[...] = xg * (conv + bias_ref[0] * zin)


def _hyena(z2d, B, lp, spectra, dft, *, cb=256):
    L = z2d.shape[0]
    P, Q, KL = spectra
    fc, fs, gc, gs = dft
    ncb = BRANCH_W // cb
    zb = NZ // cb
    cw = lp['hy_conv_w']
    cbias = lp['hy_conv_b'].reshape(1, -1)
    bias = lp['hy_bias'].reshape(HY_ORDER, 1, BRANCH_W)
    grid = (ncb, B)
    zspec = lambda part: pl.BlockSpec((L, cb), lambda j, b: (0, b * zb + part * ncb + j))
    wspec = lambda part: pl.BlockSpec((3, cb), lambda j, b: (0, part * ncb + j))
    bspec = lambda part: pl.BlockSpec((1, cb), lambda j, b: (0, part * ncb + j))
    aspec = pl.BlockSpec((L, cb), lambda j, b: (0, b * ncb + j))
    dspec = pl.BlockSpec((L, L), lambda j, b: (0, 0))
    ospec = lambda o: pl.BlockSpec((1, L, cb), lambda j, b: (o, 0, j))
    kspec = lambda o: pl.BlockSpec((1, 1, cb), lambda j, b: (o, 0, j))
    cp = _cparams(("parallel", "arbitrary"))
    act = lambda dt: jax.ShapeDtypeStruct((L, B * BRANCH_W), dt)

    zcur = None
    for o in range(HY_ORDER):
        first = o == 0
        if first:
            ins, args = [zspec(0), wspec(0), bspec(0)], [z2d, cw, cbias]
        else:
            ins, args = [aspec], [zcur]
        yr, yn = pl.pallas_call(
            functools.partial(_hy_fwd_kernel, first=first), grid=grid,
            in_specs=ins + [ospec(o), ospec(o), kspec(o), dspec, dspec],
            out_specs=[aspec, aspec], out_shape=[act(BF16), act(BF16)],
            compiler_params=cp, name=f"hyena_fwd{o}",
        )(*args, P, Q, KL, fc, fs)
        if first:
            zins, zargs = [zspec(0), wspec(0), bspec(0)], [z2d, cw, cbias]
        else:
            zins, zargs = [aspec], [zcur]
        zcur = pl.pallas_call(
            functools.partial(_hy_inv_kernel, first=first), grid=grid,
            in_specs=[aspec, aspec, dspec, dspec] + zins + [zspec(o + 1), wspec(o + 1), bspec(o + 1), kspec(o)],
            out_specs=aspec, out_shape=act(F32),
            compiler_params=cp, name=f"hyena_inv{o}",
        )(yr, yn, gc, gs, *zargs, z2d, cw, cbias, bias)
    return zcur


S5_BLK_STATES = (LANES // S5_GROUP_CH) * S5_STATE


def _s5_param_kernel(lr_ref, li_ref, ls_ref, br_ref, bi_ref, lbr_ref, lbi_ref, bbr_ref, bbi_ref):
    lr, li = lr_ref[...], li_ref[...]
    step = jnp.exp(ls_ref[...])
    mag = jnp.exp(lr * step)
    lbr = mag * jnp.cos(li * step)
    lbi = mag * jnp.sin(li * step)
    den = lr * lr + li * li
    nr = lbr - 1.0
    cr = (nr * lr + lbi * li) / den
    ci = (lbi * lr - nr * li) / den
    br, bi = br_ref[...], bi_ref[...]
    lbr_ref[...] = lbr
    lbi_ref[...] = lbi
    bbr_ref[...] = cr * br - ci * bi
    bbi_ref[...] = cr * bi + ci * br


def _s5_params(lp):
    n = 2 * S5_GROUPS * S5_STATE
    col = lambda a: a.reshape(n, 1)
    ls = jnp.broadcast_to(lp['s5_log_step'][:, :, None], (2, S5_GROUPS, S5_STATE))
    tr = 1024
    cspec = pl.BlockSpec((tr, 1), lambda i: (i, 0))
    bspec = pl.BlockSpec((tr, S5_GROUP_CH), lambda i: (i, 0))
    lbr, lbi, bbr, bbi = pl.pallas_call(
        _s5_param_kernel, grid=(n // tr,),
        in_specs=[cspec, cspec, cspec, bspec, bspec],
        out_specs=[cspec, cspec, bspec, bspec],
        out_shape=[jax.ShapeDtypeStruct((n, 1), F32)] * 2 + [jax.ShapeDtypeStruct((n, S5_GROUP_CH), F32)] * 2,
        compiler_params=_cparams(("parallel",)),
        name="s5_params",
    )(col(lp['s5_lambda_re']), col(lp['s5_lambda_im']), col(ls),
      lp['s5_B_re'].reshape(n, S5_GROUP_CH), lp['s5_B_im'].reshape(n, S5_GROUP_CH))
    gp = (2, S5_GROUPS, S5_STATE)
    return lbr.reshape(gp), lbi.reshape(gp), bbr.reshape(gp + (S5_GROUP_CH,)), bbi.reshape(gp + (S5_GROUP_CH,))


def _s5_block_weights(lbr, lbi, bbr, bbi, c_re, c_im):
    gl = LANES // S5_GROUP_CH
    nb = S5_GROUPS // gl
    eye = jnp.eye(gl, dtype=F32)

    def in_blk(bb):
        t = jnp.transpose(bb.reshape(2, nb, gl, S5_STATE, S5_GROUP_CH), (0, 1, 2, 4, 3))
        t = t[:, :, :, :, None, :] * eye[None, None, :, None, :, None]
        return t.reshape(2, nb, LANES, S5_BLK_STATES)

    def out_blk(c):
        t = jnp.transpose(c.reshape(2, nb, gl, S5_GROUP_CH, S5_STATE), (0, 1, 2, 4, 3))
        t = t[:, :, :, :, None, :] * eye[None, None, :, None, :, None]
        return t.reshape(2, nb, S5_BLK_STATES, LANES)

    w_in = jnp.concatenate([in_blk(bbr), in_blk(bbi)], axis=-1).astype(BF16)
    lam = jnp.stack([lbr.reshape(2, nb, S5_BLK_STATES), lbi.reshape(2, nb, S5_BLK_STATES)], axis=2)
    lam = jnp.transpose(lam, (1, 0, 2, 3)).reshape(nb, 4, S5_BLK_STATES)
    return w_in, lam, out_blk(c_re).astype(BF16), out_blk(c_im).astype(BF16)


def _s5_kernel(uf_ref, ub_ref, wf_ref, wb_ref, cfr_ref, cfi_ref, cbr_ref, cbi_ref, lam_ref, s0_ref,
               yf_ref, yb_ref, fin_ref, xf_scr, xb_scr, st_scr):
    tt, b, _ = uf_ref.shape
    ns = S5_BLK_STATES
    i = pl.program_id(1)

    @pl.when(i == 0)
    def _():
        st_scr[...] = s0_ref[0]

    xf_scr[...] = _dot(uf_ref[...].reshape(tt * b, LANES).astype(BF16), wf_ref[0, 0])
    xb_scr[...] = _dot(ub_ref[...].reshape(tt * b, LANES).astype(BF16), wb_ref[0, 0])
    lam = lam_ref[0]
    lfr, lfi, lbr, lbi = lam[0:1], lam[1:2], lam[2:3], lam[3:4]

    def step(t, carry):
        sfr, sfi, sbr, sbi = carry
        rf = pl.ds(pl.multiple_of(t * b, b), b)
        nfr = lfr * sfr - lfi * sfi + xf_scr[rf, :ns]
        nfi = lfr * sfi + lfi * sfr + xf_scr[rf, ns:]
        xf_scr[rf, :ns] = nfr
        xf_scr[rf, ns:] = nfi
        rb = pl.ds(pl.multiple_of((tt - 1 - t) * b, b), b)
        nbr = lbr * sbr - lbi * sbi + xb_scr[rb, :ns]
        nbi = lbr * sbi + lbi * sbr + xb_scr[rb, ns:]
        xb_scr[rb, :ns] = nbr
        xb_scr[rb, ns:] = nbi
        return nfr, nfi, nbr, nbi

    init = (st_scr[0, :, :ns], st_scr[0, :, ns:], st_scr[1, :, :ns], st_scr[1, :, ns:])
    sfr, sfi, sbr, sbi = lax.fori_loop(0, tt, step, init)
    st_scr[0, :, :ns] = sfr
    st_scr[0, :, ns:] = sfi
    st_scr[1, :, :ns] = sbr
    st_scr[1, :, ns:] = sbi

    yf = _dot(xf_scr[:, :ns].astype(BF16), cfr_ref[0, 0]) - _dot(xf_scr[:, ns:].astype(BF16), cfi_ref[0, 0])
    yb = _dot(xb_scr[:, :ns].astype(BF16), cbr_ref[0, 0]) - _dot(xb_scr[:, ns:].astype(BF16), cbi_ref[0, 0])
    yf_ref[...] = yf.reshape(tt, b, LANES)
    yb_ref[...] = yb.reshape(tt, b, LANES)

    @pl.when(i == pl.num_programs(1) - 1)
    def _():
        fin_ref[0] = st_scr[...]


def _s5(z3d, blk, s0, *, tt=64):
    L, B, _ = z3d.shape
    w_in, lam, c_re, c_im = blk
    nb = w_in.shape[1]
    tt = min(tt, L)
    nt = L // tt
    u0 = Z_S5_IN // LANES
    ns2 = 2 * S5_BLK_STATES
    wspec = lambda d: pl.BlockSpec((1, 1, LANES, ns2), lambda j, i: (d, j, 0, 0))
    cspec = lambda d: pl.BlockSpec((1, 1, S5_BLK_STATES, LANES), lambda j, i: (d, j, 0, 0))
    return pl.pallas_call(
        _s5_kernel, grid=(nb, nt),
        in_specs=[pl.BlockSpec((tt, B, LANES), lambda j, i: (i, 0, u0 + j)),
                  pl.BlockSpec((tt, B, LANES), lambda j, i: (nt - 1 - i, 0, u0 + j)),
                  wspec(0), wspec(1), cspec(0), cspec(0), cspec(1), cspec(1),
                  pl.BlockSpec((1, 4, S5_BLK_STATES), lambda j, i: (j, 0, 0)),
                  pl.BlockSpec((1, 2, B, ns2), lambda j, i: (j, 0, 0, 0))],
        out_specs=[pl.BlockSpec((tt, B, LANES), lambda j, i: (i, 0, j)),
                   pl.BlockSpec((tt, B, LANES), lambda j, i: (nt - 1 - i, 0, j)),
                   pl.BlockSpec((1, 2, B, ns2), lambda j, i: (j, 0, 0, 0))],
        out_shape=[jax.ShapeDtypeStruct((L, B, BRANCH_W), F32)] * 2
                  + [jax.ShapeDtypeStruct((nb, 2, B, ns2), F32)],
        scratch_shapes=[pltpu.VMEM((tt * B, ns2), F32), pltpu.VMEM((tt * B, ns2), F32),
                        pltpu.VMEM((2, B, ns2), F32)],
        compiler_params=_cparams(("parallel", "arbitrary")),
        name="s5_scan",
    )(z3d, z3d, w_in, w_in, c_re, c_im, c_re, c_im, lam, s0)


def _s5_state_to_blocks(s0):
    B = s0.shape[0]
    gl = LANES // S5_GROUP_CH
    nb = S5_GROUPS // gl
    t = s0.reshape(B, 2, nb, gl * S5_STATE, 2)
    t = jnp.transpose(t, (2, 1, 0, 4, 3))
    return t.reshape(nb, 2, B, 2 * S5_BLK_STATES)


def _s5_blocks_to_state(fin):
    nb, _, B, _ = fin.shape
    t = fin.reshape(nb, 2, B, 2, S5_BLK_STATES)
    t = jnp.transpose(t, (2, 1, 0, 4, 3))
    return t.reshape(B, 2, S5_GROUPS, S5_STATE, 2)


GATE_SLOTS = 8


def _gdn_gate_kernel(ab_ref, alog_ref, dtb_ref, col_ref, row_ref):
    L = ab_ref.shape[0]
    c = GDN_CHUNK
    slot = lax.broadcasted_iota(jnp.int32, (c, LANES), 1) % GATE_SLOTS
    r = lax.broadcasted_iota(jnp.int32, (c, c), 0)
    s = lax.broadcasted_iota(jnp.int32, (c, c), 1)
    lower = jnp.where(r >= s, 1.0, 0.0).astype(BF16)
    upper = jnp.where(r <= s, 1.0, 0.0).astype(BF16)
    neg_a = -jnp.exp(alog_ref[...])
    dtb = dtb_ref[...]

    def body(n, carry):
        rows = pl.ds(pl.multiple_of(n * c, c), c)
        x = ab_ref[rows, :]
        xa = x + dtb
        softplus = jnp.maximum(xa, 0.0) + jnp.log1p(jnp.exp(-jnp.abs(xa)))
        g = jnp.where((slot == 2) | (slot == 3), neg_a * softplus, 0.0)
        g1 = g.astype(BF16)
        r1 = g - g1.astype(F32)
        g2 = r1.astype(BF16)
        g3 = (r1 - g2.astype(F32)).astype(BF16)
        cf = _dot(lower, g1) + _dot(lower, g2) + _dot(lower, g3)
        cb = _dot(upper, g1) + _dot(upper, g2) + _dot(upper, g3)
        tile = jnp.where(slot < 2, jax.nn.sigmoid(x), jnp.where(slot == 2, cf, cb))
        col_ref[rows, :] = tile
        row_ref[0, pl.ds(n, 1)] = tile.T[None]
        return carry

    lax.fori_loop(0, L // c, body, 0)


def _gdn_gates(z2d, B, lp):
    L = z2d.shape[0]
    c = GDN_CHUNK
    ab0 = Z_AB // LANES
    zb = NZ // LANES

    def pad(a):
        t = jnp.zeros((GDN_HEADS, GATE_SLOTS), F32).at[:, 2:4].set(a.T)
        return jnp.pad(t.reshape(1, -1), ((0, 0), (0, LANES - GDN_HEADS * GATE_SLOTS)))

    return pl.pallas_call(
        _gdn_gate_kernel, grid=(B,),
        in_specs=[pl.BlockSpec((L, LANES), lambda b: (0, b * zb + ab0)),
                  pl.BlockSpec((1, LANES), lambda b: (0, 0)),
                  pl.BlockSpec((1, LANES), lambda b: (0, 0))],
        out_specs=[pl.BlockSpec((L, LANES), lambda b: (0, b)),
                   pl.BlockSpec((1, L // c, LANES, c), lambda b: (b, 0, 0, 0))],
        out_shape=[jax.ShapeDtypeStruct((L, B * LANES), F32),
                   jax.ShapeDtypeStruct((B, L // c, LANES, c), F32)],
        compiler_params=_cparams(("parallel",)),
        name="gdn_gates",
    )(z2d, pad(lp['gdn_A_log']), pad(lp['gdn_dt_bias']))


def _l2norm(x):
    return x * lax.rsqrt(jnp.sum(x * x, axis=-1, keepdims=True) + NORM_EPS)


def _chunk_masks():
    c = GDN_CHUNK
    r = lax.broadcasted_iota(jnp.int32, (c, c), 0)
    s = lax.broadcasted_iota(jnp.int32, (c, c), 1)
    return ((r >= s, r > s), (r <= s, r < s))


def _head_gates(col_ref, rows, h):
    tile = col_ref[rows, :]
    return pltpu.roll(tile, LANES - GATE_SLOTS * h, axis=1)[:, :GATE_SLOTS]


def _chunk_decay(gates, row_ref, ci, h, d, incl):
    beta = gates[:, d:d + 1]
    gcum = gates[:, 2 + d:3 + d]
    grow = row_ref[0, ci, pl.ds(GATE_SLOTS * h + 2 + d, 1), :]
    decay = jnp.where(incl, jnp.exp(jnp.where(incl, gcum - grow, 0.0)), 0.0)
    return beta, gcum, decay


def _dwconv3_rows(x_ref, w, r0, n):
    L = x_ref.shape[0]
    x = x_ref[pl.ds(r0, n), :]
    before = x_ref[pl.ds(jnp.maximum(r0 - 1, 0), 1), :]
    after = x_ref[pl.ds(jnp.minimum(r0 + n, L - 1), 1), :]
    before = jnp.where(r0 == 0, 0.0, before)
    after = jnp.where(r0 + n == L, 0.0, after)
    r = lax.broadcasted_iota(jnp.int32, x.shape, 0)
    prev = jnp.where(r == 0, before, pltpu.roll(x, 1, axis=0))
    nxt = jnp.where(r == n - 1, after, pltpu.roll(x, n - 1, axis=0))
    return w[0:1] * prev + w[1:2] * x + w[2:3] * nxt


def _gdn_amat_kernel(k_ref, wk_ref, col_ref, row_ref, a_ref):
    c = GDN_CHUNK
    nc = k_ref.shape[0] // c
    h = pl.program_id(1)
    wk = wk_ref[...]
    masks = _chunk_masks()

    group = 4 if nc % 4 == 0 else 1

    def body(g, carry):
        prepared = []
        for u in range(group):
            ci = g * group + u
            r0 = pl.multiple_of(ci * c, c)
            kc = _l2norm(_silu(_dwconv3_rows(k_ref, wk, r0, c)))
            prepared.append((ci, kc, _head_gates(col_ref, pl.ds(r0, c), h)))
        products = []
        for ci, kc, gates in prepared:
            for d in range(2):
                beta, _, decay = _chunk_decay(gates, row_ref, ci, h, d, masks[d][0])
                kk = lax.dot_general((kc * beta).astype(BF16), kc.astype(BF16),
                                     (((1,), (1,)), ((), ())), preferred_element_type=F32)
                products.append((ci, d, kk, decay))
        for ci, d, kk, decay in products:
            a_ref[d, 0, 0, pl.ds(ci, 1)] = jnp.where(masks[d][1], kk * decay, 0.0).astype(BF16)[None]
        return carry

    lax.fori_loop(0, nc // group, body, 0)


def _tri_inverse_kernel(a_ref, t_ref, a_scr, t_scr):
    n, _, pw = a_scr.shape
    a_scr[...] = a_ref[0].astype(F32)
    t_scr[...] = jnp.zeros(t_scr.shape, F32)

    def solve(upper):
        for i in (range(n - 1, -1, -1) if upper else range(n)):
            lo, hi = (i + 1, n) if upper else (0, i)
            c0, c1 = ((i // 8) * 8, n) if upper else (0, (i // 8 + 1) * 8)
            col = lax.broadcasted_iota(jnp.int32, (c1 - c0, pw), 0) + c0

            def inner(j, acc, i=i, c0=c0, c1=c1):
                return acc - a_scr[i, pl.ds(j, 1), :] * t_scr[j, c0:c1, :]

            t_scr[i, c0:c1, :] = lax.fori_loop(lo, hi, inner, jnp.where(col == i, 1.0, 0.0), unroll=2)

    @pl.when(pl.program_id(0) == 0)
    def _():
        solve(False)

    @pl.when(pl.program_id(0) == 1)
    def _():
        solve(True)

    t_ref[0] = t_scr[...].astype(BF16)


def _tri_inverse(a):
    _, c, _, P = a.shape
    pw = 2 * LANES if P % (2 * LANES) == 0 else LANES
    pp = -(-P // pw) * pw
    if pp != P:
        a = jnp.pad(a, ((0, 0), (0, 0), (0, 0), (0, pp - P)))
    spec = pl.BlockSpec((1, c, c, pw), lambda d, p: (d, 0, 0, p))
    t = pl.pallas_call(
        _tri_inverse_kernel, grid=(2, pp // pw),
        in_specs=[spec], out_specs=spec, out_shape=jax.ShapeDtypeStruct((2, c, c, pp), BF16),
        scratch_shapes=[pltpu.VMEM((c, c, pw), F32), pltpu.VMEM((c, c, pw), F32)],
        compiler_params=_cparams(("parallel", "parallel")),
        name="gdn_tri_inverse",
    )(a)
    return t[..., :P] if pp != P else t


def _gdn_kernel(q_ref, k_ref, v_ref, wq_ref, wk_ref, wv_ref, col_ref, row_ref, t_ref, ng_ref, s0_ref,
                o_ref, sfin_ref, mo_scr, n_scr, ou_scr, gl_scr, st_scr):
    L = q_ref.shape[0]
    c = GDN_CHUNK
    nc = L // c
    h = pl.program_id(1)
    wq, wk, wv = wq_ref[...], wk_ref[...], wv_ref[...]
    st_scr[...] = s0_ref[0, :, 0]
    masks = _chunk_masks()

    def g_last_of(gcum, d):
        return gcum[c - 1:c] if d == 0 else gcum[0:1]

    group = 2 if nc % 2 == 0 else 1

    def phase1(g, carry):
        work = []
        for u in range(group):
            ci = g * group + u
            r0 = pl.multiple_of(ci * c, c)
            qc = _l2norm(_silu(_dwconv3_rows(q_ref, wq, r0, c))) * (GDN_DK ** -0.5)
            kc = _l2norm(_silu(_dwconv3_rows(k_ref, wk, r0, c)))
            vc = _silu(_dwconv3_rows(v_ref, wv, r0, c))
            gates = _head_gates(col_ref, pl.ds(r0, c), h)
            qk_raw = lax.dot_general(qc.astype(BF16), kc.astype(BF16),
                                     (((1,), (1,)), ((), ())), preferred_element_type=F32)
            for d in range(2):
                beta, gcum, decay = _chunk_decay(gates, row_ref, ci, h, d, masks[d][0])
                t_mat = t_ref[d, 0, 0, pl.ds(ci, 1)][0]
                egc = jnp.exp(gcum)
                uw = _bdot(t_mat, jnp.concatenate([vc * beta, kc * (beta * egc)], axis=1))
                g_last = g_last_of(gcum, d)
                kdec = kc * jnp.exp(g_last - gcum)
                gl_scr[d, pl.ds(ci, 1), :] = jnp.broadcast_to(jnp.exp(g_last), (1, GDN_DV))
                work.append((ci, r0, d, uw, kdec, qk_raw * decay, qc * egc))
        results = []
        for ci, r0, d, uw, kdec, qk, qe in work:
            lhs = jnp.concatenate([kdec.T, qk], axis=0)
            results.append((ci, r0, d, qe, _bdot(lhs, uw)))
        for ci, r0, d, qe, res in results:
            n_scr[d, pl.ds(ci, 1)] = res[:GDN_DK, :GDN_DV][None]
            mo = jnp.concatenate([-res[:GDN_DK, GDN_DV:], qe - res[GDN_DK:, GDN_DV:]], axis=0)
            mo_scr[d, pl.ds(ci, 1)] = mo.astype(BF16)[None]
            ou_scr[d, pl.ds(r0, c), :] = res[GDN_DK:, :GDN_DV]
        return carry

    lax.fori_loop(0, nc // group, phase1, 0)

    def advance(ci, d):
        rows = pl.ds(pl.multiple_of(ci * c, c), c)
        state = st_scr[d]
        r = _dot(mo_scr[d, pl.ds(ci, 1)][0], state.astype(BF16))
        st_scr[d] = state * gl_scr[d, pl.ds(ci, 1), :] + r[:GDN_DK] + n_scr[d, pl.ds(ci, 1)][0]
        return rows, r[GDN_DK:]

    def phase2(n, carry):
        rows, o = advance(n, 0)
        o_ref[rows, :] = o + ou_scr[0, rows, :]
        rows, o = advance(nc - 1 - n, 1)
        ou_scr[1, rows, :] = o + ou_scr[1, rows, :]
        return carry

    lax.fori_loop(0, nc, phase2, 0)
    o = o_ref[...] + ou_scr[1]
    o_ref[...] = o * lax.rsqrt(jnp.mean(o * o, axis=-1, keepdims=True) + NORM_EPS) * ng_ref[...]
    sfin_ref[0, :, 0] = st_scr[...]


def _gdn(z2d, B, lp, s0):
    L = z2d.shape[0]
    H = GDN_HEADS
    c = GDN_CHUNK
    nc = L // c
    cols, rows = _gdn_gates(z2d, B, lp)
    zb = NZ // LANES
    q0 = Z_GDN_IN // LANES
    zspec = lambda part: pl.BlockSpec((L, LANES), lambda b, h: (0, b * zb + q0 + part * H + h))
    wspec = lambda part: pl.BlockSpec((3, LANES), lambda b, h: (0, part * H + h))
    colspec = pl.BlockSpec((L, LANES), lambda b, h: (0, b))
    rowspec = pl.BlockSpec((1, nc, LANES, c), lambda b, h: (b, 0, 0, 0))
    tspec = pl.BlockSpec((2, 1, 1, nc, c, c), lambda b, h: (0, b, h, 0, 0, 0))
    sspec = pl.BlockSpec((1, 2, 1, GDN_DK, GDN_DV), lambda b, h: (b, 0, h, 0, 0))
    cw = lp['gdn_conv_w']
    cp = _cparams(("parallel", "parallel"))
    a_mat = pl.pallas_call(
        _gdn_amat_kernel, grid=(B, H),
        in_specs=[zspec(1), wspec(1), colspec, rowspec],
        out_specs=tspec, out_shape=jax.ShapeDtypeStruct((2, B, H, nc, c, c), BF16),
        compiler_params=cp, name="gdn_amat",
    )(z2d, cw, cols, rows)
    P = B * H * nc
    a_t = jnp.swapaxes(a_mat.reshape(2, P, c * c), 1, 2).reshape(2, c, c, P)
    t_t = _tri_inverse(a_t)
    t_mat = jnp.swapaxes(t_t.reshape(2, c * c, P), 1, 2).reshape(2, B, H, nc, c, c)
    return pl.pallas_call(
        _gdn_kernel, grid=(B, H),
        in_specs=[zspec(0), zspec(1), zspec(2), wspec(0), wspec(1), wspec(2), colspec, rowspec, tspec,
                  pl.BlockSpec((1, LANES), lambda b, h: (0, 0)), sspec],
        out_specs=[pl.BlockSpec((L, LANES), lambda b, h: (0, b * H + h)), sspec],
        out_shape=[jax.ShapeDtypeStruct((L, B * BRANCH_W), F32),
                   jax.ShapeDtypeStruct((B, 2, H, GDN_DK, GDN_DV), F32)],
        scratch_shapes=[pltpu.VMEM((2, nc, GDN_DK + c, GDN_DV), BF16),
                          pltpu.VMEM((2, nc, GDN_DK, GDN_DV), F32),
                          pltpu.VMEM((2, L, GDN_DV), F32),
                          pltpu.VMEM((2, nc, GDN_DV), F32),
                          pltpu.VMEM((2, GDN_DK, GDN_DV), F32)],
        compiler_params=cp, name="gdn",
    )(z2d, z2d, z2d, cw, cw, cw, cols, rows, t_mat, lp['gdn_norm_g'].reshape(1, LANES), s0)


def _merge_kernel(x_ref, hy_ref, yf_ref, yb_ref, og_ref, hyg_ref, u_ref, s5g_ref, gdg_ref,
                  m0_ref, m1_ref, m2_ref, d_ref, gw_ref, gb_ref, wb_ref, wo_ref, gate_ref, fg_ref,
                  out_ref, *scratch, final_norm):
    tl, b, d = x_ref.shape
    rows = tl * b
    flat = lambda ref: ref[...].reshape(rows, ref.shape[-1])
    y_hy = flat(hy_ref) * _silu(flat(hyg_ref))
    s5_raw = d_ref[...] * flat(u_ref) + flat(yf_ref) + flat(yb_ref)
    gelu = 0.5 * s5_raw * (1.0 + jnp.tanh(math.sqrt(2.0 / math.pi) * (s5_raw + 0.044715 * (s5_raw * s5_raw * s5_raw))))
    glu = _dot(gelu.astype(BF16), gw_ref[...]) + gb_ref[...]
    y_s5 = glu[:, :BRANCH_W] * jax.nn.sigmoid(glu[:, BRANCH_W:]) * _silu(flat(s5g_ref))
    y_gdn = flat(og_ref) * _silu(flat(gdg_ref))
    merged = jax.nn.sigmoid(flat(m0_ref)) * _dot(y_hy.astype(BF16), wb_ref[0])
    merged += jax.nn.sigmoid(flat(m1_ref)) * _dot(y_s5.astype(BF16), wb_ref[1])
    merged += jax.nn.sigmoid(flat(m2_ref)) * _dot(y_gdn.astype(BF16), wb_ref[2])
    delta = _dot(merged.astype(BF16), wo_ref[...]).reshape(tl, b, d)
    x = x_ref[...] + gate_ref[...] * delta
    if not final_norm:
        out_ref[...] = x
    else:
        res_scr, = scratch
        x = x * lax.rsqrt(jnp.mean(x * x, axis=-1, keepdims=True) + NORM_EPS) * fg_ref[...]
        x = x.reshape(rows, d)
        nk = d // LANES
        for k in range(nk):
            res_scr[k] = x[:, k * LANES:(k + 1) * LANES]
        for i in range(b):
            out_ref[i] = jnp.concatenate([res_scr[k, pl.ds(i, tl, stride=b), :] for k in range(nk)], axis=1)


def _merge(x, hy, yf, yb, og, z3d, gate, lp, final_g, *, rows_per_tile=256):
    L, B, _ = x.shape
    tl = max(1, min(L, rows_per_tile // B))
    final = final_g is not None
    act = pl.BlockSpec((tl, B, BRANCH_W), lambda i: (i, 0, 0))
    zspec = lambda col: pl.BlockSpec((tl, B, BRANCH_W), lambda i: (i, 0, col // BRANCH_W))
    once = pl.Buffered(1)
    const = lambda *shape: pl.BlockSpec(shape, lambda i: (0,) * len(shape), pipeline_mode=once)
    return pl.pallas_call(
        functools.partial(_merge_kernel, final_norm=final),
        grid=(L // tl,),
        in_specs=[act, act, act, act, act,
                  zspec(Z_HY_GATE), zspec(Z_S5_IN), zspec(Z_S5_GATE), zspec(Z_GDN_GATE),
                  zspec(Z_MERGE), zspec(Z_MERGE + D_MODEL), zspec(Z_MERGE + 2 * D_MODEL),
                  const(1, BRANCH_W), const(BRANCH_W, 2 * BRANCH_W), const(1, 2 * BRANCH_W),
                  const(3, BRANCH_W, D_MODEL), const(D_MODEL, D_MODEL),
                  const(1, B, D_MODEL), const(1, 1, D_MODEL)],
        out_specs=pl.BlockSpec((B, tl, D_MODEL), lambda i: (0, i, 0)) if final else act,
        out_shape=jax.ShapeDtypeStruct((B, L, D_MODEL) if final else (L, B, D_MODEL), F32),
        scratch_shapes=[pltpu.VMEM((D_MODEL // LANES, tl * B, LANES), F32)] if final else [],
        compiler_params=_cparams(("parallel",)),
        name="merge",
    )(x, hy, yf, yb, og, z3d, z3d, z3d, z3d, z3d, z3d, z3d,
      lp['s5_D'].reshape(1, BRANCH_W), lp['s5_glu_w'].astype(BF16), lp['s5_glu_b'].reshape(1, -1),
      lp['w_branch'].astype(BF16), lp['w_out'].astype(BF16), gate[None],
      (final_g if final_g is not None else jnp.ones((D_MODEL,), F32)).reshape(1, 1, D_MODEL))


def _pack_w_in(w_in):
    ab = w_in[:, W_IN_AB:W_IN_AB + 4 * GDN_HEADS].reshape(D_MODEL, 4, GDN_HEADS)
    ab = jnp.pad(jnp.transpose(ab, (0, 2, 1)), ((0, 0), (0, 0), (0, GATE_SLOTS - 4)))
    ab = ab.reshape(D_MODEL, GDN_HEADS * GATE_SLOTS)
    pad = jnp.zeros((D_MODEL, NZ - Z_AB - GDN_HEADS * GATE_SLOTS), w_in.dtype)
    return jnp.concatenate([w_in[:, :W_IN_AB], w_in[:, W_IN_AB + 4 * GDN_HEADS:], ab, pad], axis=1).astype(BF16)


def _trunk_layer(x, pos, mod, lp, shared, s0_gdn, s0_s5, final_g, *, entry):
    B, L = (x.shape[0], x.shape[1]) if entry else (x.shape[1], x.shape[0])
    mod3 = jnp.transpose(mod.reshape(B, 3, D_MODEL), (1, 0, 2))
    if entry:
        z3d, x = _in_projection_entry(x, pos, mod3, lp['norm_g'], shared['w_packed'])
    else:
        z3d = _in_projection(x, mod3, lp['norm_g'], shared['w_packed'])
    z2d = z3d.reshape(L, B * NZ)
    tables = shared['tables'][L]
    spectra = _hyena_filter_spectra(L, lp, (tables['zpos'], tables['decay'], tables['dft'][0], tables['dft'][1]))
    hy = _hyena(z2d, B, lp, spectra, tables['dft'])
    yf, yb, s5_fin = _s5(z3d, shared['s5_blk'], _s5_state_to_blocks(s0_s5))
    og, gdn_fin = _gdn(z2d, B, lp, s0_gdn)
    x = _merge(x, hy.reshape(L, B, BRANCH_W), yf, yb, og.reshape(L, B, BRANCH_W), z3d, mod3[2], lp, final_g)
    return x, gdn_fin, _s5_blocks_to_state(s5_fin)


def _grid_pos_embed(n_tokens, dim):
    rows = n_tokens // GRID_W
    t = jnp.arange(rows * GRID_W)
    r = (t // GRID_W).astype(F32)
    col = (t % GRID_W).astype(F32)
    quarter = dim // 4
    omega = 1.0 / (10000.0 ** (jnp.arange(quarter, dtype=F32) / quarter))
    er = r[:, None] * omega
    ec = col[:, None] * omega
    return jnp.concatenate([jnp.sin(er), jnp.cos(er), jnp.sin(ec), jnp.cos(ec)], axis=-1)


_LAYER_KEYS = ('norm_g', 'w_in', 'hy_conv_w', 'hy_conv_b', 'hy_f_w1', 'hy_f_b1', 'hy_f_w2', 'hy_f_b2',
               'hy_f_w3', 'hy_f_freq', 'hy_bias', 's5_lambda_re', 's5_lambda_im', 's5_log_step',
               's5_B_re', 's5_B_im', 's5_C_re', 's5_C_im', 's5_D', 's5_glu_w', 's5_glu_b', 'gdn_conv_w',
               'gdn_A_log', 'gdn_dt_bias', 'gdn_norm_g', 'w_branch', 'w_out')


def kernel(x_prompt, x_sample, c, c_ctx, state_gdn, state_s5, w_mod, b_mod, norm_g, w_in, hy_conv_w, hy_conv_b, hy_f_w1, hy_f_b1, hy_f_w2, hy_f_b2, hy_f_w3, hy_f_freq, hy_bias, s5_lambda_re, s5_lambda_im, s5_log_step, s5_B_re, s5_B_im, s5_C_re, s5_C_im, s5_D, s5_glu_w, s5_glu_b, gdn_conv_w, gdn_A_log, gdn_dt_bias, gdn_norm_g, w_branch, w_out, final_norm_g):
    params = dict(norm_g=norm_g, w_in=w_in, hy_conv_w=hy_conv_w, hy_conv_b=hy_conv_b, hy_f_w1=hy_f_w1,
                  hy_f_b1=hy_f_b1, hy_f_w2=hy_f_w2, hy_f_b2=hy_f_b2, hy_f_w3=hy_f_w3, hy_f_freq=hy_f_freq,
                  hy_bias=hy_bias, s5_lambda_re=s5_lambda_re, s5_lambda_im=s5_lambda_im,
                  s5_log_step=s5_log_step, s5_B_re=s5_B_re, s5_B_im=s5_B_im, s5_C_re=s5_C_re,
                  s5_C_im=s5_C_im, s5_D=s5_D, s5_glu_w=s5_glu_w, s5_glu_b=s5_glu_b, gdn_conv_w=gdn_conv_w,
                  gdn_A_log=gdn_A_log, gdn_dt_bias=gdn_dt_bias, gdn_norm_g=gdn_norm_g, w_branch=w_branch,
                  w_out=w_out)
    depth = w_in.shape[0]
    bp, lc, _ = x_prompt.shape
    bl, ll, _ = x_sample.shape

    cc = jnp.concatenate([c_ctx[None], c], axis=0)
    n_rows = -(-cc.shape[0] // 8) * 8
    cc = jnp.pad(cc, ((0, n_rows - cc.shape[0]), (0, 0)))
    mods = _modulation(cc, w_mod, b_mod)

    tables = {}
    for L in {lc, ll}:
        zpos, decay = _filter_tables(L)
        tables[L] = dict(zpos=zpos, decay=decay, dft=_dft_tables(L))

    xc, xl = x_prompt, x_sample
    pos = _grid_pos_embed(ll, D_MODEL).astype(x_sample.dtype)
    zero_gdn = jnp.zeros((bp, 2, GDN_HEADS, GDN_DK, GDN_DV), F32)
    zero_s5 = jnp.zeros((bp, 2, S5_GROUPS, S5_STATE, 2), F32)
    gdn_states, s5_states = [], []
    for l in range(depth):
        lp = {k: params[k][l] for k in _LAYER_KEYS}
        lbr, lbi, bbr, bbi = _s5_params(lp)
        shared = dict(w_packed=_pack_w_in(lp['w_in']), tables=tables,
                      s5_blk=_s5_block_weights(lbr, lbi, bbr, bbi, lp['s5_C_re'], lp['s5_C_im']))
        fg = final_norm_g if l == depth - 1 else None
        mod_ctx = jnp.broadcast_to(mods[l, 0:1], (bp, 3 * D_MODEL))
        mod_lat = mods[l, 1:1 + bl]
        xc, fin_gdn, fin_s5 = _trunk_layer(xc, None, mod_ctx, lp, shared, zero_gdn, zero_s5, fg, entry=l == 0)
        gdn_states.append(fin_gdn)
        s5_states.append(fin_s5)
        xl, _, _ = _trunk_layer(xl, pos if l == 0 else None, mod_lat, lp, shared,
                                state_gdn[:, l].astype(F32), state_s5[:, l].astype(F32), fg, entry=l == 0)
    return (xc, xl, jnp.stack(gdn_states, axis=1), jnp.stack(s5_states, axis=1))
```

```python
import functools
import math

import jax
import jax.numpy as jnp
from jax import lax
from jax.experimental import pallas as pl
from jax.experimental.pallas import tpu as pltpu

F32 = jnp.float32
BF16 = jnp.bfloat16

D_MODEL = 1024
BRANCH_W = 1024
GRID_W = 64
HY_ORDER = 2
HY_EMB = 33
HY_HID = 64
HY_DECAY_TARGET = 1e-2
HY_SHORT_DECAY_PCT = 0.3
HY_LONG_DECAY_PCT = 1.5
S5_GROUP_CH = 16
S5_GROUPS = 64
S5_STATE = 64
GDN_DK = 128
GDN_DV = 128
GDN_HEADS = 8
GDN_CHUNK = 64
NORM_EPS = 1e-6

LANES = 128
Z_HY_IN, Z_HY_GATE, Z_S5_IN, Z_S5_GATE = 0, 3072, 4096, 5120
Z_GDN_IN, Z_GDN_GATE, Z_MERGE, Z_AB = 6144, 9216, 10240, 13312
NZ = 14336
W_IN_AB = 9216
VMEM_LIMIT_V7X = 56 * 1024 * 1024


def _cparams(sem):
    return pltpu.CompilerParams(dimension_semantics=sem, vmem_limit_bytes=VMEM_LIMIT_V7X)


def _dot(a, b):
    return jnp.dot(a, b, preferred_element_type=F32)


def _bdot(a, b):
    return jnp.dot(a.astype(BF16), b.astype(BF16), preferred_element_type=F32)


def _split(a):
    hi = a.astype(BF16)
    lo = (a - hi.astype(F32)).astype(BF16)
    return hi, lo


def _dot3(a, b):
    ah, al = _split(a)
    bh, bl = _split(b)
    return _dot(ah, bh) + _dot(al, bh) + _dot(ah, bl)


def _silu(x):
    return x * jax.nn.sigmoid(x)


def _shift_prev(x):
    r = lax.broadcasted_iota(jnp.int32, x.shape, 0)
    return jnp.where(r == 0, 0.0, pltpu.roll(x, 1, axis=0))


def _shift_next(x):
    n = x.shape[0]
    r = lax.broadcasted_iota(jnp.int32, x.shape, 0)
    return jnp.where(r == n - 1, 0.0, pltpu.roll(x, n - 1, axis=0))


def _dwconv3(x, w):
    return w[0:1] * _shift_prev(x) + w[1:2] * x + w[2:3] * _shift_next(x)


def _mod_kernel(c_ref, w_ref, b_ref, o_ref):
    o_ref[0] = _dot3(_silu(c_ref[...]), w_ref[0]) + b_ref[0]


def _modulation(cc, w_mod, b_mod):
    depth = w_mod.shape[0]
    r = cc.shape[0]
    return pl.pallas_call(
        _mod_kernel,
        grid=(depth, 3),
        in_specs=[pl.BlockSpec((r, D_MODEL), lambda l, n: (0, 0)),
                  pl.BlockSpec((1, D_MODEL, D_MODEL), lambda l, n: (l, 0, n)),
                  pl.BlockSpec((1, 1, D_MODEL), lambda l, n: (l, 0, n))],
        out_specs=pl.BlockSpec((1, r, D_MODEL), lambda l, n: (l, 0, n)),
        out_shape=jax.ShapeDtypeStruct((depth, r, 3 * D_MODEL), F32),
        compiler_params=_cparams(("arbitrary", "arbitrary")),
        name="modulation",
    )(cc, w_mod, b_mod.reshape(depth, 1, 3 * D_MODEL))


def _inproj_kernel(*refs, entry, has_pos):
    refs = list(refs)
    x_ref = refs.pop(0)
    pos_ref = refs.pop(0) if has_pos else None
    mod_ref, g_ref, w_ref, z_ref = refs[:4]
    xres_ref = refs[4] if entry else None
    h_scr = refs[-1]
    b = pl.program_id(0)

    @pl.when(pl.program_id(2) == 0)
    def _():
        x = x_ref[0] if entry else x_ref[...]
        if has_pos:
            x = x + pos_ref[...]
        if entry:
            xres_ref[...] = x
        y = x * lax.rsqrt(jnp.mean(x * x, axis=-1, keepdims=True) + NORM_EPS) * g_ref[...]
        h_scr[...] = (y * (1.0 + mod_ref[1, pl.ds(b, 1), :]) + mod_ref[0, pl.ds(b, 1), :]).astype(BF16)

    z_ref[...] = _dot(h_scr[...], w_ref[...])


def _in_projection(x, pos, mod3, norm_g, w_packed, *, entry, rows_per_tile=1024, tn=2048):
    B = mod3.shape[1]
    L = x.shape[1] if entry else x.shape[0]
    has_pos = pos is not None
    tl = min(L, rows_per_tile // 2 if has_pos else rows_per_tile)
    nj = NZ // tn
    if entry:
        in_specs = [pl.BlockSpec((1, tl, D_MODEL), lambda b, i, j: (b, i, 0))]
    else:
        in_specs = [pl.BlockSpec((tl, D_MODEL), lambda b, i, j: (i, b))]
    args = [x]
    if has_pos:
        in_specs.append(pl.BlockSpec((tl, D_MODEL), lambda b, i, j: (i, 0)))
        args.append(pos)
    in_specs += [pl.BlockSpec((3, B, D_MODEL), lambda b, i, j: (0, 0, 0)),
                 pl.BlockSpec((1, D_MODEL), lambda b, i, j: (0, 0)),
                 pl.BlockSpec((D_MODEL, tn), lambda b, i, j: (0, j))]
    args += [mod3, norm_g.reshape(1, D_MODEL), w_packed]
    out_specs = [pl.BlockSpec((tl, tn), lambda b, i, j: (i, b * nj + j))]
    out_shape = [jax.ShapeDtypeStruct((L, B * NZ), F32)]
    if entry:
        out_specs.append(pl.BlockSpec((tl, D_MODEL), lambda b, i, j: (i, b)))
        out_shape.append(jax.ShapeDtypeStruct((L, B * D_MODEL), F32))
    res = pl.pallas_call(
        functools.partial(_inproj_kernel, entry=entry, has_pos=has_pos),
        grid=(B, L // tl, nj), in_specs=in_specs, out_specs=out_specs, out_shape=out_shape,
        scratch_shapes=[pltpu.VMEM((tl, D_MODEL), BF16)],
        compiler_params=_cparams(("parallel", "parallel", "arbitrary")),
        name="in_projection",
    )(*args)
    return (res[0], res[1]) if entry else (res[0], x)


def _dft_tables(L):
    f = lax.broadcasted_iota(jnp.int32, (L, L), 0)
    t = lax.broadcasted_iota(jnp.int32, (L, L), 1)
    ang = ((f * t) % (2 * L)).astype(F32) * (math.pi / L)
    c = jnp.cos(ang).astype(BF16)
    s = jnp.sin(ang)
    fwd_s = jnp.where(f == 0, jnp.where(t % 2 == 0, 1.0, -1.0), s).astype(BF16)
    inv_s = jnp.where(t == 0, jnp.where(f % 2 == 0, 1.0, -1.0), s).astype(BF16)
    return c, fwd_s, c, inv_s


def _filter_tables(L):
    t = jnp.linspace(0.0, 1.0, L, dtype=F32)[:, None]
    bands = (HY_EMB - 1) // 2
    fb = jnp.linspace(1e-4, bands - 1, bands, dtype=F32)[None, :]
    wpos = 2.0 * math.pi * jnp.arange(L, dtype=F32)[:, None] / L
    zpos = jnp.concatenate([t, jnp.cos(fb * wpos), -jnp.sin(fb * wpos)], axis=-1)
    deltas = jnp.abs(jnp.linspace(math.log(HY_DECAY_TARGET) / HY_LONG_DECAY_PCT,
                                  math.log(HY_DECAY_TARGET) / HY_SHORT_DECAY_PCT, BRANCH_W, dtype=F32))
    decay = jnp.exp(-t * deltas)
    return zpos, decay


def _hy_filter_kernel(zpos_ref, w1_ref, b1_ref, w2_ref, b2_ref, freq_ref, w3_ref, decay_ref,
                      fc_ref, fs_ref, p_ref, q_ref, kl_ref):
    freq = freq_ref[...]
    hdn = jnp.sin(freq * (_dot3(zpos_ref[...], w1_ref[...]) + b1_ref[...]))
    hdn = jnp.sin(freq * (_dot3(hdn, w2_ref[...]) + b2_ref[...]))
    decay = decay_ref[...]
    L = decay.shape[0]
    row = lax.broadcasted_iota(jnp.int32, decay.shape, 0)
    alt = jnp.where(row % 2 == 0, 1.0, -1.0)
    bin_w = jnp.where(row == 0, 1.0, 2.0) / (2.0 * L)
    for o in range(HY_ORDER):
        fwd = _dot3(hdn, w3_ref[2 * o]) * decay
        bwd = jnp.where(row == 0, 0.0, _dot3(hdn, w3_ref[2 * o + 1]) * decay)
        norm = jnp.sum(jnp.abs(fwd) + jnp.abs(bwd), axis=0, keepdims=True)
        a = (fwd + bwd) / norm
        s = (fwd - bwd) / norm
        p_ref[o] = _bdot(fc_ref[...], a) * bin_w
        q_ref[o] = jnp.where(row == 0, 0.0, _bdot(fs_ref[...], s) * bin_w)
        kl_ref[o] = jnp.sum(alt * a, axis=0, keepdims=True) / (2.0 * L)


def _hyena_filter_spectra(L, lp, tables, *, cb=256):
    zpos, decay, fc, fs = tables
    w3 = jnp.transpose(lp['hy_f_w3'].reshape(HY_HID, HY_ORDER * 2, BRANCH_W), (1, 0, 2))
    const = lambda *shape: pl.BlockSpec(shape, lambda j: (0,) * len(shape))
    return pl.pallas_call(
        _hy_filter_kernel,
        grid=(BRANCH_W // cb,),
        in_specs=[const(L, HY_EMB), const(HY_EMB, HY_HID), const(1, HY_HID), const(HY_HID, HY_HID),
                  const(1, HY_HID), const(1, HY_HID),
                  pl.BlockSpec((HY_ORDER * 2, HY_HID, cb), lambda j: (0, 0, j)),
                  pl.BlockSpec((L, cb), lambda j: (0, j)),
                  const(L, L), const(L, L)],
        out_specs=[pl.BlockSpec((HY_ORDER, L, cb), lambda j: (0, 0, j)),
                   pl.BlockSpec((HY_ORDER, L, cb), lambda j: (0, 0, j)),
                   pl.BlockSpec((HY_ORDER, 1, cb), lambda j: (0, 0, j))],
        out_shape=[jax.ShapeDtypeStruct((HY_ORDER, L, BRANCH_W), F32),
                   jax.ShapeDtypeStruct((HY_ORDER, L, BRANCH_W), F32),
                   jax.ShapeDtypeStruct((HY_ORDER, 1, BRANCH_W), F32)],
        compiler_params=_cparams(("arbitrary",)),
        name="hyena_filter",
    )(zpos, lp['hy_f_w1'], lp['hy_f_b1'].reshape(1, HY_HID), lp['hy_f_w2'], lp['hy_f_b2'].reshape(1, HY_HID),
      lp['hy_f_freq'].reshape(1, HY_HID), w3, decay, fc, fs)


def _hy_fwd_kernel(*refs, first):
    if first:
        zin_ref, cw_ref, cbias_ref, p_ref, q_ref, kl_ref, fc_ref, fs_ref, yr_ref, yn_ref = refs
        zin = _dwconv3(zin_ref[...], cw_ref[...]) + cbias_ref[...]
    else:
        zin_ref, p_ref, q_ref, kl_ref, fc_ref, fs_ref, yr_ref, yn_ref = refs
        zin = zin_ref[...]
    zb = zin.astype(BF16)
    zr = _dot(fc_ref[...], zb)
    w = _dot(fs_ref[...], zb)
    p = p_ref[0]
    q = q_ref[0]
    row = lax.broadcasted_iota(jnp.int32, p.shape, 0)
    pd = jnp.where(row == 0, kl_ref[0], p)
    yr_ref[...] = (zr * p - w * q).astype(BF16)
    yn_ref[...] = (zr * q + w * pd).astype(BF16)


def _hy_inv_kernel(*refs, first):
    if first:
        (yr_ref, yn_ref, gc_ref, gs_ref, zin_ref, cwz_ref, cbz_ref, xg_ref, cwx_ref, cbx_ref,
         bias_ref, out_ref) = refs
        zin = _dwconv3(zin_ref[...], cwz_ref[...]) + cbz_ref[...]
    else:
        yr_ref, yn_ref, gc_ref, gs_ref, zin_ref, xg_ref, cwx_ref, cbx_ref, bias_ref, out_ref = refs
        zin = zin_ref[...]
    conv = _dot(gc_ref[...], yr_ref[...]) + _dot(gs_ref[...], yn_ref[...])
    xg = _dwconv3(xg_ref[...], cwx_ref[...]) + cbx_ref[...]
    out_ref[...] = xg * (conv + bias_ref[0] * zin)


def _hyena(z2d, B, lp, spectra, dft, *, cb=256):
    L = z2d.shape[0]
    P, Q, KL = spectra
    fc, fs, gc, gs = dft
    ncb = BRANCH_W // cb
    zb = NZ // cb
    cw = lp['hy_conv_w']
    cbias = lp['hy_conv_b'].reshape(1, -1)
    bias = lp['hy_bias'].reshape(HY_ORDER, 1, BRANCH_W)
    grid = (ncb, B)
    zspec = lambda part: pl.BlockSpec((L, cb), lambda j, b: (0, b * zb + part * ncb + j))
    wspec = lambda part: pl.BlockSpec((3, cb), lambda j, b: (0, part * ncb + j))
    bspec = lambda part: pl.BlockSpec((1, cb), lambda j, b: (0, part * ncb + j))
    aspec = pl.BlockSpec((L, cb), lambda j, b: (0, b * ncb + j))
    dspec = pl.BlockSpec((L, L), lambda j, b: (0, 0))
    ospec = lambda o: pl.BlockSpec((1, L, cb), lambda j, b: (o, 0, j))
    kspec = lambda o: pl.BlockSpec((1, 1, cb), lambda j, b: (o, 0, j))
    cp = _cparams(("parallel", "arbitrary"))
    act = lambda dt: jax.ShapeDtypeStruct((L, B * BRANCH_W), dt)

    zcur = None
    for o in range(HY_ORDER):
        first = o == 0
        if first:
            ins, args = [zspec(0), wspec(0), bspec(0)], [z2d, cw, cbias]
        else:
            ins, args = [aspec], [zcur]
        yr, yn = pl.pallas_call(
            functools.partial(_hy_fwd_kernel, first=first), grid=grid,
            in_specs=ins + [ospec(o), ospec(o), kspec(o), dspec, dspec],
            out_specs=[aspec, aspec], out_shape=[act(BF16), act(BF16)],
            compiler_params=cp, name=f"hyena_fwd{o}",
        )(*args, P, Q, KL, fc, fs)
        if first:
            zins, zargs = [zspec(0), wspec(0), bspec(0)], [z2d, cw, cbias]
        else:
            zins, zargs = [aspec], [zcur]
        zcur = pl.pallas_call(
            functools.partial(_hy_inv_kernel, first=first), grid=grid,
            in_specs=[aspec, aspec, dspec, dspec] + zins + [zspec(o + 1), wspec(o + 1), bspec(o + 1), kspec(o)],
            out_specs=aspec, out_shape=act(F32),
            compiler_params=cp, name=f"hyena_inv{o}",
        )(yr, yn, gc, gs, *zargs, z2d, cw, cbias, bias)
    return zcur


S5_BLK_STATES = (LANES // S5_GROUP_CH) * S5_STATE


def _s5_param_kernel(lr_ref, li_ref, ls_ref, br_ref, bi_ref, lbr_ref, lbi_ref, bbr_ref, bbi_ref):
    lr, li = lr_ref[...], li_ref[...]
    step = jnp.exp(ls_ref[...])
    mag = jnp.exp(lr * step)
    lbr = mag * jnp.cos(li * step)
    lbi = mag * jnp.sin(li * step)
    den = lr * lr + li * li
    nr = lbr - 1.0
    cr = (nr * lr + lbi * li) / den
    ci = (lbi * lr - nr * li) / den
    br, bi = br_ref[...], bi_ref[...]
    lbr_ref[...] = lbr
    lbi_ref[...] = lbi
    bbr_ref[...] = cr * br - ci * bi
    bbi_ref[...] = cr * bi + ci * br


def _s5_params(lp):
    n = 2 * S5_GROUPS * S5_STATE
    col = lambda a: a.reshape(n, 1)
    ls = jnp.broadcast_to(lp['s5_log_step'][:, :, None], (2, S5_GROUPS, S5_STATE))
    tr = 1024
    cspec = pl.BlockSpec((tr, 1), lambda i: (i, 0))
    bspec = pl.BlockSpec((tr, S5_GROUP_CH), lambda i: (i, 0))
    lbr, lbi, bbr, bbi = pl.pallas_call(
        _s5_param_kernel, grid=(n // tr,),
        in_specs=[cspec, cspec, cspec, bspec, bspec],
        out_specs=[cspec, cspec, bspec, bspec],
        out_shape=[jax.ShapeDtypeStruct((n, 1), F32)] * 2 + [jax.ShapeDtypeStruct((n, S5_GROUP_CH), F32)] * 2,
        compiler_params=_cparams(("parallel",)),
        name="s5_params",
    )(col(lp['s5_lambda_re']), col(lp['s5_lambda_im']), col(ls),
      lp['s5_B_re'].reshape(n, S5_GROUP_CH), lp['s5_B_im'].reshape(n, S5_GROUP_CH))
    gp = (2, S5_GROUPS, S5_STATE)
    return lbr.reshape(gp), lbi.reshape(gp), bbr.reshape(gp + (S5_GROUP_CH,)), bbi.reshape(gp + (S5_GROUP_CH,))


def _s5_block_weights(lbr, lbi, bbr, bbi, c_re, c_im):
    gl = LANES // S5_GROUP_CH
    nb = S5_GROUPS // gl
    eye = jnp.eye(gl, dtype=F32)

    def in_blk(bb):
        t = jnp.transpose(bb.reshape(2, nb, gl, S5_STATE, S5_GROUP_CH), (0, 1, 2, 4, 3))
        t = t[:, :, :, :, None, :] * eye[None, None, :, None, :, None]
        return t.reshape(2, nb, LANES, S5_BLK_STATES)

    def out_blk(c):
        t = jnp.transpose(c.reshape(2, nb, gl, S5_GROUP_CH, S5_STATE), (0, 1, 2, 4, 3))
        t = t[:, :, :, :, None, :] * eye[None, None, :, None, :, None]
        return t.reshape(2, nb, S5_BLK_STATES, LANES)

    w_in = jnp.concatenate([in_blk(bbr), in_blk(bbi)], axis=-1).astype(BF16)
    lam = jnp.stack([lbr.reshape(2, nb, S5_BLK_STATES), lbi.reshape(2, nb, S5_BLK_STATES)], axis=2)
    lam = jnp.transpose(lam, (1, 0, 2, 3)).reshape(nb, 4, S5_BLK_STATES)
    return w_in, lam, out_blk(c_re).astype(BF16), out_blk(c_im).astype(BF16)


def _s5_kernel(uf_ref, ub_ref, wf_ref, wb_ref, cfr_ref, cfi_ref, cbr_ref, cbi_ref, lam_ref, s0_ref,
               yf_ref, yb_ref, fin_ref, xf_scr, xb_scr, st_scr):
    tt, b, _ = uf_ref.shape
    ns = S5_BLK_STATES
    i = pl.program_id(1)

    @pl.when(i == 0)
    def _():
        st_scr[...] = s0_ref[0]

    xf_scr[...] = _dot(uf_ref[...].reshape(tt * b, LANES).astype(BF16), wf_ref[0, 0])
    xb_scr[...] = _dot(ub_ref[...].reshape(tt * b, LANES).astype(BF16), wb_ref[0, 0])
    lam = lam_ref[0]
    lfr, lfi, lbr, lbi = lam[0:1], lam[1:2], lam[2:3], lam[3:4]

    def step(t, carry):
        sfr, sfi, sbr, sbi = carry
        rf = pl.ds(pl.multiple_of(t * b, b), b)
        nfr = lfr * sfr - lfi * sfi + xf_scr[rf, :ns]
        nfi = lfr * sfi + lfi * sfr + xf_scr[rf, ns:]
        xf_scr[rf, :ns] = nfr
        xf_scr[rf, ns:] = nfi
        rb = pl.ds(pl.multiple_of((tt - 1 - t) * b, b), b)
        nbr = lbr * sbr - lbi * sbi + xb_scr[rb, :ns]
        nbi = lbr * sbi + lbi * sbr + xb_scr[rb, ns:]
        xb_scr[rb, :ns] = nbr
        xb_scr[rb, ns:] = nbi
        return nfr, nfi, nbr, nbi

    init = (st_scr[0, :, :ns], st_scr[0, :, ns:], st_scr[1, :, :ns], st_scr[1, :, ns:])
    sfr, sfi, sbr, sbi = lax.fori_loop(0, tt, step, init)
    st_scr[0, :, :ns] = sfr
    st_scr[0, :, ns:] = sfi
    st_scr[1, :, :ns] = sbr
    st_scr[1, :, ns:] = sbi

    yf = _dot(xf_scr[:, :ns].astype(BF16), cfr_ref[0, 0]) - _dot(xf_scr[:, ns:].astype(BF16), cfi_ref[0, 0])
    yb = _dot(xb_scr[:, :ns].astype(BF16), cbr_ref[0, 0]) - _dot(xb_scr[:, ns:].astype(BF16), cbi_ref[0, 0])
    yf_ref[...] = yf.reshape(tt, b, LANES)
    yb_ref[...] = yb.reshape(tt, b, LANES)

    @pl.when(i == pl.num_programs(1) - 1)
    def _():
        fin_ref[0] = st_scr[...]


def _s5(z3d, blk, s0, *, tt=64):
    L, B, _ = z3d.shape
    w_in, lam, c_re, c_im = blk
    nb = w_in.shape[1]
    tt = min(tt, L)
    nt = L // tt
    u0 = 0
    ns2 = 2 * S5_BLK_STATES
    wspec = lambda d: pl.BlockSpec((1, 1, LANES, ns2), lambda j, i: (d, j, 0, 0))
    cspec = lambda d: pl.BlockSpec((1, 1, S5_BLK_STATES, LANES), lambda j, i: (d, j, 0, 0))
    return pl.pallas_call(
        _s5_kernel, grid=(nb, nt),
        in_specs=[pl.BlockSpec((tt, B, LANES), lambda j, i: (i, 0, u0 + j)),
                  pl.BlockSpec((tt, B, LANES), lambda j, i: (nt - 1 - i, 0, u0 + j)),
                  wspec(0), wspec(1), cspec(0), cspec(0), cspec(1), cspec(1),
                  pl.BlockSpec((1, 4, S5_BLK_STATES), lambda j, i: (j, 0, 0)),
                  pl.BlockSpec((1, 2, B, ns2), lambda j, i: (j, 0, 0, 0))],
        out_specs=[pl.BlockSpec((tt, B, LANES), lambda j, i: (i, 0, j)),
                   pl.BlockSpec((tt, B, LANES), lambda j, i: (nt - 1 - i, 0, j)),
                   pl.BlockSpec((1, 2, B, ns2), lambda j, i: (j, 0, 0, 0))],
        out_shape=[jax.ShapeDtypeStruct((L, B, BRANCH_W), F32)] * 2
                  + [jax.ShapeDtypeStruct((nb, 2, B, ns2), F32)],
        scratch_shapes=[pltpu.VMEM((tt * B, ns2), F32), pltpu.VMEM((tt * B, ns2), F32),
                        pltpu.VMEM((2, B, ns2), F32)],
        compiler_params=_cparams(("parallel", "arbitrary")),
        name="s5_scan",
    )(z3d, z3d, w_in, w_in, c_re, c_im, c_re, c_im, lam, s0)


def _s5_state_to_blocks(s0):
    B = s0.shape[0]
    gl = LANES // S5_GROUP_CH
    nb = S5_GROUPS // gl
    t = s0.reshape(B, 2, nb, gl * S5_STATE, 2)
    t = jnp.transpose(t, (2, 1, 0, 4, 3))
    return t.reshape(nb, 2, B, 2 * S5_BLK_STATES)


def _s5_blocks_to_state(fin):
    nb, _, B, _ = fin.shape
    t = fin.reshape(nb, 2, B, 2, S5_BLK_STATES)
    t = jnp.transpose(t, (2, 1, 0, 4, 3))
    return t.reshape(B, 2, S5_GROUPS, S5_STATE, 2)


GATE_SLOTS = 8


def _gdn_gate_kernel(ab_ref, alog_ref, dtb_ref, col_ref, row_ref):
    L = ab_ref.shape[0]
    c = GDN_CHUNK
    slot = lax.broadcasted_iota(jnp.int32, (c, LANES), 1) % GATE_SLOTS
    r = lax.broadcasted_iota(jnp.int32, (c, c), 0)
    s = lax.broadcasted_iota(jnp.int32, (c, c), 1)
    lower = jnp.where(r >= s, 1.0, 0.0).astype(BF16)
    upper = jnp.where(r <= s, 1.0, 0.0).astype(BF16)
    neg_a = -jnp.exp(alog_ref[...])
    dtb = dtb_ref[...]

    def body(n, carry):
        rows = pl.ds(pl.multiple_of(n * c, c), c)
        x = ab_ref[rows, :]
        xa = x + dtb
        softplus = jnp.maximum(xa, 0.0) + jnp.log1p(jnp.exp(-jnp.abs(xa)))
        g = jnp.where((slot == 2) | (slot == 3), neg_a * softplus, 0.0)
        g1 = g.astype(BF16)
        r1 = g - g1.astype(F32)
        g2 = r1.astype(BF16)
        g3 = (r1 - g2.astype(F32)).astype(BF16)
        cf = _dot(lower, g1) + _dot(lower, g2) + _dot(lower, g3)
        cb = _dot(upper, g1) + _dot(upper, g2) + _dot(upper, g3)
        tile = jnp.where(slot < 2, jax.nn.sigmoid(x), jnp.where(slot == 2, cf, cb))
        col_ref[rows, :] = tile
        row_ref[0, pl.ds(n, 1)] = tile.T[None]
        return carry

    lax.fori_loop(0, L // c, body, 0)


def _gdn_gates(z2d, B, lp):
    L = z2d.shape[0]
    c = GDN_CHUNK
    ab0 = Z_AB // LANES
    zb = NZ // LANES

    def pad(a):
        t = jnp.zeros((GDN_HEADS, GATE_SLOTS), F32).at[:, 2:4].set(a.T)
        return jnp.pad(t.reshape(1, -1), ((0, 0), (0, LANES - GDN_HEADS * GATE_SLOTS)))

    return pl.pallas_call(
        _gdn_gate_kernel, grid=(B,),
        in_specs=[pl.BlockSpec((L, LANES), lambda b: (0, b * zb + ab0)),
                  pl.BlockSpec((1, LANES), lambda b: (0, 0)),
                  pl.BlockSpec((1, LANES), lambda b: (0, 0))],
        out_specs=[pl.BlockSpec((L, LANES), lambda b: (0, b)),
                   pl.BlockSpec((1, L // c, LANES, c), lambda b: (b, 0, 0, 0))],
        out_shape=[jax.ShapeDtypeStruct((L, B * LANES), F32),
                   jax.ShapeDtypeStruct((B, L // c, LANES, c), F32)],
        compiler_params=_cparams(("parallel",)),
        name="gdn_gates",
    )(z2d, pad(lp['gdn_A_log']), pad(lp['gdn_dt_bias']))


def _l2norm(x):
    return x * lax.rsqrt(jnp.sum(x * x, axis=-1, keepdims=True) + NORM_EPS)


def _chunk_masks():
    c = GDN_CHUNK
    r = lax.broadcasted_iota(jnp.int32, (c, c), 0)
    s = lax.broadcasted_iota(jnp.int32, (c, c), 1)
    return ((r >= s, r > s), (r <= s, r < s))


def _head_gates(col_ref, rows, h):
    tile = col_ref[rows, :]
    return pltpu.roll(tile, LANES - GATE_SLOTS * h, axis=1)[:, :GATE_SLOTS]


def _chunk_decay(gates, row_ref, ci, h, d, incl):
    beta = gates[:, d:d + 1]
    gcum = gates[:, 2 + d:3 + d]
    grow = row_ref[0, ci, pl.ds(GATE_SLOTS * h + 2 + d, 1), :]
    decay = jnp.where(incl, jnp.exp(jnp.where(incl, gcum - grow, 0.0)), 0.0)
    return beta, gcum, decay


def _dwconv3_rows(x_ref, w, r0, n):
    L = x_ref.shape[0]
    x = x_ref[pl.ds(r0, n), :]
    before = x_ref[pl.ds(jnp.maximum(r0 - 1, 0), 1), :]
    after = x_ref[pl.ds(jnp.minimum(r0 + n, L - 1), 1), :]
    before = jnp.where(r0 == 0, 0.0, before)
    after = jnp.where(r0 + n == L, 0.0, after)
    r = lax.broadcasted_iota(jnp.int32, x.shape, 0)
    prev = jnp.where(r == 0, before, pltpu.roll(x, 1, axis=0))
    nxt = jnp.where(r == n - 1, after, pltpu.roll(x, n - 1, axis=0))
    return w[0:1] * prev + w[1:2] * x + w[2:3] * nxt


def _gdn_amat_kernel(k_ref, wk_ref, col_ref, row_ref, a_ref):
    c = GDN_CHUNK
    nc = k_ref.shape[0] // c
    h = pl.program_id(1)
    wk = wk_ref[...]
    masks = _chunk_masks()

    group = 4 if nc % 4 == 0 else 1

    def body(g, carry):
        prepared = []
        for u in range(group):
            ci = g * group + u
            r0 = pl.multiple_of(ci * c, c)
            kc = _l2norm(_silu(_dwconv3_rows(k_ref, wk, r0, c)))
            prepared.append((ci, kc, _head_gates(col_ref, pl.ds(r0, c), h)))
        products = []
        for ci, kc, gates in prepared:
            kb16 = kc.astype(BF16)
            kk = lax.dot_general(kb16, kb16, (((1,), (1,)), ((), ())), preferred_element_type=F32)
            products.append((ci, kk, gates))
        for ci, kk, gates in products:
            for d in range(2):
                beta, _, decay = _chunk_decay(gates, row_ref, ci, h, d, masks[d][0])
                a_ref[d, 0, 0, pl.ds(ci, 1)] = jnp.where(masks[d][1], kk * (decay * beta), 0.0).astype(BF16)[None]
        return carry

    lax.fori_loop(0, nc // group, body, 0)


def _tri_inverse_kernel(a_ref, t_ref, a_scr, t_scr):
    n, _, pw = a_scr.shape
    a_scr[...] = a_ref[0].astype(F32)
    t_scr[...] = jnp.zeros(t_scr.shape, F32)

    def solve(upper):
        for i in (range(n - 1, -1, -1) if upper else range(n)):
            lo, hi = (i + 1, n) if upper else (0, i)
            c0, c1 = ((i // 8) * 8, n) if upper else (0, (i // 8 + 1) * 8)
            col = lax.broadcasted_iota(jnp.int32, (c1 - c0, pw), 0) + c0

            def inner(j, acc, i=i, c0=c0, c1=c1):
                return acc - a_scr[i, pl.ds(j, 1), :] * t_scr[j, c0:c1, :]

            t_scr[i, c0:c1, :] = lax.fori_loop(lo, hi, inner, jnp.where(col == i, 1.0, 0.0), unroll=2)

    @pl.when(pl.program_id(0) == 0)
    def _():
        solve(False)

    @pl.when(pl.program_id(0) == 1)
    def _():
        solve(True)

    t_ref[0] = t_scr[...].astype(BF16)


def _tri_inverse(a):
    _, c, _, P = a.shape
    pw = 2 * LANES if P % (2 * LANES) == 0 else LANES
    pp = -(-P // pw) * pw
    if pp != P:
        a = jnp.pad(a, ((0, 0), (0, 0), (0, 0), (0, pp - P)))
    spec = pl.BlockSpec((1, c, c, pw), lambda d, p: (d, 0, 0, p))
    t = pl.pallas_call(
        _tri_inverse_kernel, grid=(2, pp // pw),
        in_specs=[spec], out_specs=spec, out_shape=jax.ShapeDtypeStruct((2, c, c, pp), BF16),
        scratch_shapes=[pltpu.VMEM((c, c, pw), F32), pltpu.VMEM((c, c, pw), F32)],
        compiler_params=_cparams(("parallel", "parallel")),
        name="gdn_tri_inverse",
    )(a)
    return t[..., :P] if pp != P else t


def _gdn_kernel(q_ref, k_ref, v_ref, wq_ref, wk_ref, wv_ref, col_ref, row_ref, t_ref, ng_ref, s0_ref,
                o_ref, sfin_ref, mo_scr, n_scr, ou_scr, gl_scr, st_scr):
    L = q_ref.shape[0]
    c = GDN_CHUNK
    nc = L // c
    h = pl.program_id(1)
    wq, wk, wv = wq_ref[...], wk_ref[...], wv_ref[...]
    st_scr[...] = s0_ref[0, :, 0]
    masks = _chunk_masks()

    def g_last_of(gcum, d):
        return gcum[c - 1:c] if d == 0 else gcum[0:1]

    group = 2 if nc % 2 == 0 else 1

    def phase1(g, carry):
        work = []
        for u in range(group):
            ci = g * group + u
            r0 = pl.multiple_of(ci * c, c)
            qc = _l2norm(_silu(_dwconv3_rows(q_ref, wq, r0, c))) * (GDN_DK ** -0.5)
            kc = _l2norm(_silu(_dwconv3_rows(k_ref, wk, r0, c)))
            vc = _silu(_dwconv3_rows(v_ref, wv, r0, c))
            gates = _head_gates(col_ref, pl.ds(r0, c), h)
            kb16 = kc.astype(BF16)
            qk_raw = lax.dot_general(qc.astype(BF16), kb16, (((1,), (1,)), ((), ())), preferred_element_type=F32)
            k_t = kc.T
            for d in range(2):
                _, gcum, decay = _chunk_decay(gates, row_ref, ci, h, d, masks[d][0])
                beta_row = row_ref[0, ci, pl.ds(GATE_SLOTS * h + d, 1), :]
                gc_row = row_ref[0, ci, pl.ds(GATE_SLOTS * h + 2 + d, 1), :]
                t_mat = t_ref[d, 0, 0, pl.ds(ci, 1)][0].astype(F32)
                u = _bdot(t_mat * beta_row, vc)
                w = _bdot(t_mat * (beta_row * jnp.exp(gc_row)), kb16)
                g_last = g_last_of(gcum, d)
                kdec_t = k_t * jnp.exp(g_last - gc_row)
                gl_scr[d, pl.ds(ci, 1), :] = jnp.broadcast_to(jnp.exp(g_last), (1, GDN_DV))
                work.append((ci, r0, d, jnp.concatenate([u, w], axis=1), kdec_t, qk_raw * decay,
                             qc * jnp.exp(gcum)))
        results = []
        for ci, r0, d, uw, kdec_t, qk, qe in work:
            lhs = jnp.concatenate([kdec_t, qk], axis=0)
            results.append((ci, r0, d, qe, _bdot(lhs, uw)))
        for ci, r0, d, qe, res in results:
            n_scr[d, pl.ds(ci, 1)] = res[:GDN_DK, :GDN_DV][None]
            mo = jnp.concatenate([-res[:GDN_DK, GDN_DV:], qe - res[GDN_DK:, GDN_DV:]], axis=0)
            mo_scr[d, pl.ds(ci, 1)] = mo.astype(BF16)[None]
            ou_scr[d, pl.ds(r0, c), :] = res[GDN_DK:, :GDN_DV]
        return carry

    lax.fori_loop(0, nc // group, phase1, 0)

    def advance(ci, d):
        rows = pl.ds(pl.multiple_of(ci * c, c), c)
        state = st_scr[d]
        r = _dot(mo_scr[d, pl.ds(ci, 1)][0], state.astype(BF16))
        st_scr[d] = state * gl_scr[d, pl.ds(ci, 1), :] + r[:GDN_DK] + n_scr[d, pl.ds(ci, 1)][0]
        return rows, r[GDN_DK:]

    def phase2(n, carry):
        rows, o = advance(n, 0)
        o_ref[rows, :] = o + ou_scr[0, rows, :]
        rows, o = advance(nc - 1 - n, 1)
        ou_scr[1, rows, :] = o + ou_scr[1, rows, :]
        return carry

    lax.fori_loop(0, nc, phase2, 0)
    o = o_ref[...] + ou_scr[1]
    o_ref[...] = o * lax.rsqrt(jnp.mean(o * o, axis=-1, keepdims=True) + NORM_EPS) * ng_ref[...]
    sfin_ref[0, :, 0] = st_scr[...]


def _gdn(z2d, B, lp, s0):
    L = z2d.shape[0]
    H = GDN_HEADS
    c = GDN_CHUNK
    nc = L // c
    cols, rows = _gdn_gates(z2d, B, lp)
    zb = NZ // LANES
    q0 = Z_GDN_IN // LANES
    zspec = lambda part: pl.BlockSpec((L, LANES), lambda b, h: (0, b * zb + q0 + part * H + h))
    wspec = lambda part: pl.BlockSpec((3, LANES), lambda b, h: (0, part * H + h))
    colspec = pl.BlockSpec((L, LANES), lambda b, h: (0, b))
    rowspec = pl.BlockSpec((1, nc, LANES, c), lambda b, h: (b, 0, 0, 0))
    tspec = pl.BlockSpec((2, 1, 1, nc, c, c), lambda b, h: (0, b, h, 0, 0, 0))
    sspec = pl.BlockSpec((1, 2, 1, GDN_DK, GDN_DV), lambda b, h: (b, 0, h, 0, 0))
    cw = lp['gdn_conv_w']
    cp = _cparams(("parallel", "parallel"))
    a_mat = pl.pallas_call(
        _gdn_amat_kernel, grid=(B, H),
        in_specs=[zspec(1), wspec(1), colspec, rowspec],
        out_specs=tspec, out_shape=jax.ShapeDtypeStruct((2, B, H, nc, c, c), BF16),
        compiler_params=cp, name="gdn_amat",
    )(z2d, cw, cols, rows)
    P = B * H * nc
    a_t = jnp.swapaxes(a_mat.reshape(2, P, c * c), 1, 2).reshape(2, c, c, P)
    t_t = _tri_inverse(a_t)
    t_mat = jnp.swapaxes(t_t.reshape(2, c * c, P), 1, 2).reshape(2, B, H, nc, c, c)
    return pl.pallas_call(
        _gdn_kernel, grid=(B, H),
        in_specs=[zspec(0), zspec(1), zspec(2), wspec(0), wspec(1), wspec(2), colspec, rowspec, tspec,
                  pl.BlockSpec((1, LANES), lambda b, h: (0, 0)), sspec],
        out_specs=[pl.BlockSpec((L, LANES), lambda b, h: (0, b * H + h)), sspec],
        out_shape=[jax.ShapeDtypeStruct((L, B * BRANCH_W), F32),
                   jax.ShapeDtypeStruct((B, 2, H, GDN_DK, GDN_DV), F32)],
        scratch_shapes=[pltpu.VMEM((2, nc, GDN_DK + c, GDN_DV), BF16),
                          pltpu.VMEM((2, nc, GDN_DK, GDN_DV), F32),
                          pltpu.VMEM((2, L, GDN_DV), F32),
                          pltpu.VMEM((2, nc, GDN_DV), F32),
                          pltpu.VMEM((2, GDN_DK, GDN_DV), F32)],
        compiler_params=cp, name="gdn",
    )(z2d, z2d, z2d, cw, cw, cw, cols, rows, t_mat, lp['gdn_norm_g'].reshape(1, LANES), s0)


def _merge_kernel(x_ref, hy_ref, yf_ref, yb_ref, og_ref, hyg_ref, u_ref, s5g_ref, gdg_ref,
                  m0_ref, m1_ref, m2_ref, d_ref, gw_ref, gb_ref, wb_ref, wo_ref, gate_ref, fg_ref,
                  out_ref, *, final_norm):
    b = pl.program_id(0)
    y_hy = hy_ref[...] * _silu(hyg_ref[...])
    s5_raw = d_ref[...] * u_ref[...] + yf_ref[...] + yb_ref[...]
    gelu = 0.5 * s5_raw * (1.0 + jnp.tanh(math.sqrt(2.0 / math.pi) * (s5_raw + 0.044715 * (s5_raw * s5_raw * s5_raw))))
    glu = _dot(gelu.astype(BF16), gw_ref[...]) + gb_ref[...]
    y_s5 = glu[:, :BRANCH_W] * jax.nn.sigmoid(glu[:, BRANCH_W:]) * _silu(s5g_ref[...])
    y_gdn = og_ref[...] * _silu(gdg_ref[...])
    merged = jax.nn.sigmoid(m0_ref[...]) * _dot(y_hy.astype(BF16), wb_ref[0])
    merged += jax.nn.sigmoid(m1_ref[...]) * _dot(y_s5.astype(BF16), wb_ref[1])
    merged += jax.nn.sigmoid(m2_ref[...]) * _dot(y_gdn.astype(BF16), wb_ref[2])
    x = x_ref[...] + gate_ref[pl.ds(b, 1), :] * _dot(merged.astype(BF16), wo_ref[...])
    if final_norm:
        out_ref[0] = x * lax.rsqrt(jnp.mean(x * x, axis=-1, keepdims=True) + NORM_EPS) * fg_ref[...]
    else:
        out_ref[...] = x


def _merge(x, hy, yf, yb, og, z2d, gate, lp, final_g, *, rows_per_tile=256):
    L = x.shape[0]
    B = gate.shape[0]
    tl = min(L, rows_per_tile)
    final = final_g is not None
    zb = NZ // BRANCH_W
    act = pl.BlockSpec((tl, BRANCH_W), lambda b, i: (i, b))
    zspec = lambda col: pl.BlockSpec((tl, BRANCH_W), lambda b, i: (i, b * zb + col // BRANCH_W))
    once = pl.Buffered(1)
    const = lambda *shape: pl.BlockSpec(shape, lambda b, i: (0,) * len(shape), pipeline_mode=once)
    return pl.pallas_call(
        functools.partial(_merge_kernel, final_norm=final),
        grid=(B, L // tl),
        in_specs=[act, act, act, act, act,
                  zspec(Z_HY_GATE), zspec(Z_S5_IN), zspec(Z_S5_GATE), zspec(Z_GDN_GATE),
                  zspec(Z_MERGE), zspec(Z_MERGE + D_MODEL), zspec(Z_MERGE + 2 * D_MODEL),
                  const(1, BRANCH_W), const(BRANCH_W, 2 * BRANCH_W), const(1, 2 * BRANCH_W),
                  const(3, BRANCH_W, D_MODEL), const(D_MODEL, D_MODEL),
                  const(B, D_MODEL), const(1, D_MODEL)],
        out_specs=pl.BlockSpec((1, tl, D_MODEL), lambda b, i: (b, i, 0)) if final else act,
        out_shape=jax.ShapeDtypeStruct((B, L, D_MODEL) if final else (L, B * D_MODEL), F32),
        compiler_params=_cparams(("parallel", "parallel")),
        name="merge",
    )(x, hy, yf, yb, og, z2d, z2d, z2d, z2d, z2d, z2d, z2d,
      lp['s5_D'].reshape(1, BRANCH_W), lp['s5_glu_w'].astype(BF16), lp['s5_glu_b'].reshape(1, -1),
      lp['w_branch'].astype(BF16), lp['w_out'].astype(BF16), gate,
      (final_g if final_g is not None else jnp.ones((D_MODEL,), F32)).reshape(1, D_MODEL))


def _pack_w_in(w_in):
    ab = w_in[:, W_IN_AB:W_IN_AB + 4 * GDN_HEADS].reshape(D_MODEL, 4, GDN_HEADS)
    ab = jnp.pad(jnp.transpose(ab, (0, 2, 1)), ((0, 0), (0, 0), (0, GATE_SLOTS - 4)))
    ab = ab.reshape(D_MODEL, GDN_HEADS * GATE_SLOTS)
    pad = jnp.zeros((D_MODEL, NZ - Z_AB - GDN_HEADS * GATE_SLOTS), w_in.dtype)
    return jnp.concatenate([w_in[:, :W_IN_AB], w_in[:, W_IN_AB + 4 * GDN_HEADS:], ab, pad], axis=1).astype(BF16)


def _trunk_layer(x, pos, mod, lp, shared, s0_gdn, s0_s5, final_g, *, entry):
    B = mod.shape[0]
    mod3 = jnp.transpose(mod.reshape(B, 3, D_MODEL), (1, 0, 2))
    z2d, x = _in_projection(x, pos, mod3, lp['norm_g'], shared['w_packed'], entry=entry)
    L = z2d.shape[0]
    tables = shared['tables'][L]
    spectra = _hyena_filter_spectra(L, lp, (tables['zpos'], tables['decay'], tables['dft'][0], tables['dft'][1]))
    hy = _hyena(z2d, B, lp, spectra, tables['dft'])
    s5_u = jnp.stack([z2d[:, b * NZ + Z_S5_IN:b * NZ + Z_S5_IN + BRANCH_W] for b in range(B)], axis=1)
    yf, yb, s5_fin = _s5(s5_u, shared['s5_blk'], _s5_state_to_blocks(s0_s5))
    og, gdn_fin = _gdn(z2d, B, lp, s0_gdn)
    x = _merge(x, hy, yf.reshape(L, B * BRANCH_W), yb.reshape(L, B * BRANCH_W), og, z2d, mod3[2], lp, final_g)
    return x, gdn_fin, _s5_blocks_to_state(s5_fin)


def _grid_pos_embed(n_tokens, dim):
    rows = n_tokens // GRID_W
    t = jnp.arange(rows * GRID_W)
    r = (t // GRID_W).astype(F32)
    col = (t % GRID_W).astype(F32)
    quarter = dim // 4
    omega = 1.0 / (10000.0 ** (jnp.arange(quarter, dtype=F32) / quarter))
    er = r[:, None] * omega
    ec = col[:, None] * omega
    return jnp.concatenate([jnp.sin(er), jnp.cos(er), jnp.sin(ec), jnp.cos(ec)], axis=-1)


_LAYER_KEYS = ('norm_g', 'w_in', 'hy_conv_w', 'hy_conv_b', 'hy_f_w1', 'hy_f_b1', 'hy_f_w2', 'hy_f_b2',
               'hy_f_w3', 'hy_f_freq', 'hy_bias', 's5_lambda_re', 's5_lambda_im', 's5_log_step',
               's5_B_re', 's5_B_im', 's5_C_re', 's5_C_im', 's5_D', 's5_glu_w', 's5_glu_b', 'gdn_conv_w',
               'gdn_A_log', 'gdn_dt_bias', 'gdn_norm_g', 'w_branch', 'w_out')


def kernel(x_prompt, x_sample, c, c_ctx, state_gdn, state_s5, w_mod, b_mod, norm_g, w_in, hy_conv_w, hy_conv_b, hy_f_w1, hy_f_b1, hy_f_w2, hy_f_b2, hy_f_w3, hy_f_freq, hy_bias, s5_lambda_re, s5_lambda_im, s5_log_step, s5_B_re, s5_B_im, s5_C_re, s5_C_im, s5_D, s5_glu_w, s5_glu_b, gdn_conv_w, gdn_A_log, gdn_dt_bias, gdn_norm_g, w_branch, w_out, final_norm_g):
    params = dict(norm_g=norm_g, w_in=w_in, hy_conv_w=hy_conv_w, hy_conv_b=hy_conv_b, hy_f_w1=hy_f_w1,
                  hy_f_b1=hy_f_b1, hy_f_w2=hy_f_w2, hy_f_b2=hy_f_b2, hy_f_w3=hy_f_w3, hy_f_freq=hy_f_freq,
                  hy_bias=hy_bias, s5_lambda_re=s5_lambda_re, s5_lambda_im=s5_lambda_im,
                  s5_log_step=s5_log_step, s5_B_re=s5_B_re, s5_B_im=s5_B_im, s5_C_re=s5_C_re,
                  s5_C_im=s5_C_im, s5_D=s5_D, s5_glu_w=s5_glu_w, s5_glu_b=s5_glu_b, gdn_conv_w=gdn_conv_w,
                  gdn_A_log=gdn_A_log, gdn_dt_bias=gdn_dt_bias, gdn_norm_g=gdn_norm_g, w_branch=w_branch,
                  w_out=w_out)
    depth = w_in.shape[0]
    bp, lc, _ = x_prompt.shape
    bl, ll, _ = x_sample.shape

    cc = jnp.concatenate([c_ctx[None], c], axis=0)
    n_rows = -(-cc.shape[0] // 8) * 8
    cc = jnp.pad(cc, ((0, n_rows - cc.shape[0]), (0, 0)))
    mods = _modulation(cc, w_mod, b_mod)

    tables = {}
    for L in {lc, ll}:
        zpos, decay = _filter_tables(L)
        tables[L] = dict(zpos=zpos, decay=decay, dft=_dft_tables(L))

    xc, xl = x_prompt, x_sample
    pos = _grid_pos_embed(ll, D_MODEL).astype(x_sample.dtype)
    zero_gdn = jnp.zeros((bp, 2, GDN_HEADS, GDN_DK, GDN_DV), F32)
    zero_s5 = jnp.zeros((bp, 2, S5_GROUPS, S5_STATE, 2), F32)
    gdn_states, s5_states = [], []
    for l in range(depth):
        lp = {k: params[k][l] for k in _LAYER_KEYS}
        lbr, lbi, bbr, bbi = _s5_params(lp)
        shared = dict(w_packed=_pack_w_in(lp['w_in']), tables=tables,
                      s5_blk=_s5_block_weights(lbr, lbi, bbr, bbi, lp['s5_C_re'], lp['s5_C_im']))
        fg = final_norm_g if l == depth - 1 else None
        mod_ctx = jnp.broadcast_to(mods[l, 0:1], (bp, 3 * D_MODEL))
        mod_lat = mods[l, 1:1 + bl]
        xc, fin_gdn, fin_s5 = _trunk_layer(xc, None, mod_ctx, lp, shared, zero_gdn, zero_s5, fg, entry=l == 0)
        gdn_states.append(fin_gdn)
        s5_states.append(fin_s5)
        xl, _, _ = _trunk_layer(xl, pos if l == 0 else None, mod_lat, lp, shared,
                                state_gdn[:, l].astype(F32), state_s5[:, l].astype(F32), fg, entry=l == 0)
    return (xc, xl, jnp.stack(gdn_states, axis=1), jnp.stack(s5_states, axis=1))
```

```python
import functools
import math

import jax
import jax.numpy as jnp
from jax import lax
from jax.experimental import pallas as pl
from jax.experimental.pallas import tpu as pltpu

F32 = jnp.float32
BF16 = jnp.bfloat16

D_MODEL = 1024
BRANCH_W = 1024
GRID_W = 64
HY_ORDER = 2
HY_EMB = 33
HY_HID = 64
HY_DECAY_TARGET = 1e-2
HY_SHORT_DECAY_PCT = 0.3
HY_LONG_DECAY_PCT = 1.5
S5_GROUP_CH = 16
S5_GROUPS = 64
S5_STATE = 64
GDN_DK = 128
GDN_DV = 128
GDN_HEADS = 8
GDN_CHUNK = 64
NORM_EPS = 1e-6

LANES = 128
Z_HY_IN, Z_HY_GATE, Z_S5_IN, Z_S5_GATE = 0, 3072, 4096, 5120
Z_GDN_IN, Z_GDN_GATE, Z_MERGE, Z_AB = 6144, 9216, 10240, 13312
NZ = 14336
W_IN_AB = 9216
VMEM_LIMIT_V7X = 56 * 1024 * 1024


def _cparams(sem):
    return pltpu.CompilerParams(dimension_semantics=sem, vmem_limit_bytes=VMEM_LIMIT_V7X)


def _dot(a, b):
    return jnp.dot(a, b, preferred_element_type=F32)


def _bdot(a, b):
    return jnp.dot(a.astype(BF16), b.astype(BF16), preferred_element_type=F32)


def _split(a):
    hi = a.astype(BF16)
    lo = (a - hi.astype(F32)).astype(BF16)
    return hi, lo


def _dot3(a, b):
    ah, al = _split(a)
    bh, bl = _split(b)
    return _dot(ah, bh) + _dot(al, bh) + _dot(ah, bl)


def _silu(x):
    return x * jax.nn.sigmoid(x)


def _shift_prev(x):
    r = lax.broadcasted_iota(jnp.int32, x.shape, 0)
    return jnp.where(r == 0, 0.0, pltpu.roll(x, 1, axis=0))


def _shift_next(x):
    n = x.shape[0]
    r = lax.broadcasted_iota(jnp.int32, x.shape, 0)
    return jnp.where(r == n - 1, 0.0, pltpu.roll(x, n - 1, axis=0))


def _dwconv3(x, w):
    return w[0:1] * _shift_prev(x) + w[1:2] * x + w[2:3] * _shift_next(x)


def _mod_kernel(c_ref, w_ref, b_ref, o_ref):
    o_ref[0] = _dot3(_silu(c_ref[...]), w_ref[0]) + b_ref[0]


def _modulation(cc, w_mod, b_mod):
    depth = w_mod.shape[0]
    r = cc.shape[0]
    return pl.pallas_call(
        _mod_kernel,
        grid=(depth, 3),
        in_specs=[pl.BlockSpec((r, D_MODEL), lambda l, n: (0, 0)),
                  pl.BlockSpec((1, D_MODEL, D_MODEL), lambda l, n: (l, 0, n)),
                  pl.BlockSpec((1, 1, D_MODEL), lambda l, n: (l, 0, n))],
        out_specs=pl.BlockSpec((1, r, D_MODEL), lambda l, n: (l, 0, n)),
        out_shape=jax.ShapeDtypeStruct((depth, r, 3 * D_MODEL), F32),
        compiler_params=_cparams(("arbitrary", "arbitrary")),
        name="modulation",
    )(cc, w_mod, b_mod.reshape(depth, 1, 3 * D_MODEL))


def _inproj_kernel(*refs, entry, has_pos):
    refs = list(refs)
    x_ref = refs.pop(0)
    pos_ref = refs.pop(0) if has_pos else None
    mod_ref, g_ref, w_ref, z_ref = refs[:4]
    xres_ref = refs[4] if entry else None
    h_scr = refs[-1]
    b = pl.program_id(0)

    @pl.when(pl.program_id(2) == 0)
    def _():
        x = x_ref[0] if entry else x_ref[...]
        if has_pos:
            x = x + pos_ref[...]
        if entry:
            xres_ref[...] = x
        y = x * lax.rsqrt(jnp.mean(x * x, axis=-1, keepdims=True) + NORM_EPS) * g_ref[...]
        h_scr[...] = (y * (1.0 + mod_ref[1, pl.ds(b, 1), :]) + mod_ref[0, pl.ds(b, 1), :]).astype(BF16)

    z_ref[...] = _dot(h_scr[...], w_ref[...])


def _in_projection(x, pos, mod3, norm_g, w_packed, *, entry, rows_per_tile=1024, tn=2048):
    B = mod3.shape[1]
    L = x.shape[1] if entry else x.shape[0]
    has_pos = pos is not None
    tl = min(L, rows_per_tile)
    if entry and tl == rows_per_tile:
        tn //= 2
    nj = NZ // tn
    if entry:
        in_specs = [pl.BlockSpec((1, tl, D_MODEL), lambda b, i, j: (b, i, 0))]
    else:
        in_specs = [pl.BlockSpec((tl, D_MODEL), lambda b, i, j: (i, b))]
    args = [x]
    if has_pos:
        in_specs.append(pl.BlockSpec((tl, D_MODEL), lambda b, i, j: (i, 0)))
        args.append(pos)
    in_specs += [pl.BlockSpec((3, B, D_MODEL), lambda b, i, j: (0, 0, 0)),
                 pl.BlockSpec((1, D_MODEL), lambda b, i, j: (0, 0)),
                 pl.BlockSpec((D_MODEL, tn), lambda b, i, j: (0, j))]
    args += [mod3, norm_g.reshape(1, D_MODEL), w_packed]
    out_specs = [pl.BlockSpec((tl, tn), lambda b, i, j: (i, b * nj + j))]
    out_shape = [jax.ShapeDtypeStruct((L, B * NZ), F32)]
    if entry:
        out_specs.append(pl.BlockSpec((tl, D_MODEL), lambda b, i, j: (i, b)))
        out_shape.append(jax.ShapeDtypeStruct((L, B * D_MODEL), F32))
    res = pl.pallas_call(
        functools.partial(_inproj_kernel, entry=entry, has_pos=has_pos),
        grid=(B, L // tl, nj), in_specs=in_specs, out_specs=out_specs, out_shape=out_shape,
        scratch_shapes=[pltpu.VMEM((tl, D_MODEL), BF16)],
        compiler_params=_cparams(("parallel", "parallel", "arbitrary")),
        name="in_projection",
    )(*args)
    return (res[0], res[1]) if entry else (res[0], x)


def _dft_tables(L):
    f = lax.broadcasted_iota(jnp.int32, (L, L), 0)
    t = lax.broadcasted_iota(jnp.int32, (L, L), 1)
    ang = ((f * t) % (2 * L)).astype(F32) * (math.pi / L)
    c = jnp.cos(ang).astype(BF16)
    s = jnp.sin(ang)
    fwd_s = jnp.where(f == 0, jnp.where(t % 2 == 0, 1.0, -1.0), s).astype(BF16)
    inv_s = jnp.where(t == 0, jnp.where(f % 2 == 0, 1.0, -1.0), s).astype(BF16)
    return c, fwd_s, c, inv_s


def _filter_tables(L):
    t = jnp.linspace(0.0, 1.0, L, dtype=F32)[:, None]
    bands = (HY_EMB - 1) // 2
    fb = jnp.linspace(1e-4, bands - 1, bands, dtype=F32)[None, :]
    wpos = 2.0 * math.pi * jnp.arange(L, dtype=F32)[:, None] / L
    zpos = jnp.concatenate([t, jnp.cos(fb * wpos), -jnp.sin(fb * wpos)], axis=-1)
    deltas = jnp.abs(jnp.linspace(math.log(HY_DECAY_TARGET) / HY_LONG_DECAY_PCT,
                                  math.log(HY_DECAY_TARGET) / HY_SHORT_DECAY_PCT, BRANCH_W, dtype=F32))
    decay = jnp.exp(-t * deltas)
    return zpos, decay


def _hy_filter_kernel(zpos_ref, w1_ref, b1_ref, w2_ref, b2_ref, freq_ref, w3_ref, decay_ref,
                      fc_ref, fs_ref, p_ref, q_ref, kl_ref):
    freq = freq_ref[...]
    hdn = jnp.sin(freq * (_dot3(zpos_ref[...], w1_ref[...]) + b1_ref[...]))
    hdn = jnp.sin(freq * (_dot3(hdn, w2_ref[...]) + b2_ref[...]))
    decay = decay_ref[...]
    L = decay.shape[0]
    row = lax.broadcasted_iota(jnp.int32, decay.shape, 0)
    alt = jnp.where(row % 2 == 0, 1.0, -1.0)
    bin_w = jnp.where(row == 0, 1.0, 2.0) / (2.0 * L)
    for o in range(HY_ORDER):
        fwd = _dot3(hdn, w3_ref[2 * o]) * decay
        bwd = jnp.where(row == 0, 0.0, _dot3(hdn, w3_ref[2 * o + 1]) * decay)
        norm = jnp.sum(jnp.abs(fwd) + jnp.abs(bwd), axis=0, keepdims=True)
        a = (fwd + bwd) / norm
        s = (fwd - bwd) / norm
        p_ref[o] = _bdot(fc_ref[...], a) * bin_w
        q_ref[o] = jnp.where(row == 0, 0.0, _bdot(fs_ref[...], s) * bin_w)
        kl_ref[o] = jnp.sum(alt * a, axis=0, keepdims=True) / (2.0 * L)


def _hyena_filter_spectra(L, lp, tables, *, cb=256):
    zpos, decay, fc, fs = tables
    w3 = jnp.transpose(lp['hy_f_w3'].reshape(HY_HID, HY_ORDER * 2, BRANCH_W), (1, 0, 2))
    const = lambda *shape: pl.BlockSpec(shape, lambda j: (0,) * len(shape))
    return pl.pallas_call(
        _hy_filter_kernel,
        grid=(BRANCH_W // cb,),
        in_specs=[const(L, HY_EMB), const(HY_EMB, HY_HID), const(1, HY_HID), const(HY_HID, HY_HID),
                  const(1, HY_HID), const(1, HY_HID),
                  pl.BlockSpec((HY_ORDER * 2, HY_HID, cb), lambda j: (0, 0, j)),
                  pl.BlockSpec((L, cb), lambda j: (0, j)),
                  const(L, L), const(L, L)],
        out_specs=[pl.BlockSpec((HY_ORDER, L, cb), lambda j: (0, 0, j)),
                   pl.BlockSpec((HY_ORDER, L, cb), lambda j: (0, 0, j)),
                   pl.BlockSpec((HY_ORDER, 1, cb), lambda j: (0, 0, j))],
        out_shape=[jax.ShapeDtypeStruct((HY_ORDER, L, BRANCH_W), F32),
                   jax.ShapeDtypeStruct((HY_ORDER, L, BRANCH_W), F32),
                   jax.ShapeDtypeStruct((HY_ORDER, 1, BRANCH_W), F32)],
        compiler_params=_cparams(("arbitrary",)),
        name="hyena_filter",
    )(zpos, lp['hy_f_w1'], lp['hy_f_b1'].reshape(1, HY_HID), lp['hy_f_w2'], lp['hy_f_b2'].reshape(1, HY_HID),
      lp['hy_f_freq'].reshape(1, HY_HID), w3, decay, fc, fs)


def _hy_fwd_kernel(*refs, first):
    if first:
        zin_ref, cw_ref, cbias_ref, p_ref, q_ref, kl_ref, fc_ref, fs_ref, yr_ref, yn_ref = refs
        zin = _dwconv3(zin_ref[...], cw_ref[...]) + cbias_ref[...]
    else:
        zin_ref, p_ref, q_ref, kl_ref, fc_ref, fs_ref, yr_ref, yn_ref = refs
        zin = zin_ref[...]
    zb = zin.astype(BF16)
    zr = _dot(fc_ref[...], zb)
    w = _dot(fs_ref[...], zb)
    p = p_ref[0]
    q = q_ref[0]
    row = lax.broadcasted_iota(jnp.int32, p.shape, 0)
    pd = jnp.where(row == 0, kl_ref[0], p)
    yr_ref[...] = (zr * p - w * q).astype(BF16)
    yn_ref[...] = (zr * q + w * pd).astype(BF16)


def _hy_inv_kernel(*refs, first):
    if first:
        (yr_ref, yn_ref, gc_ref, gs_ref, zin_ref, cwz_ref, cbz_ref, xg_ref, cwx_ref, cbx_ref,
         bias_ref, out_ref) = refs
        zin = _dwconv3(zin_ref[...], cwz_ref[...]) + cbz_ref[...]
    else:
        yr_ref, yn_ref, gc_ref, gs_ref, zin_ref, xg_ref, cwx_ref, cbx_ref, bias_ref, out_ref = refs
        zin = zin_ref[...]
    conv = _dot(gc_ref[...], yr_ref[...]) + _dot(gs_ref[...], yn_ref[...])
    xg = _dwconv3(xg_ref[...], cwx_ref[...]) + cbx_ref[...]
    out_ref[...] = xg * (conv + bias_ref[0] * zin)


def _hyena(z2d, B, lp, spectra, dft, *, cb=256):
    L = z2d.shape[0]
    P, Q, KL = spectra
    fc, fs, gc, gs = dft
    ncb = BRANCH_W // cb
    zb = NZ // cb
    cw = lp['hy_conv_w']
    cbias = lp['hy_conv_b'].reshape(1, -1)
    bias = lp['hy_bias'].reshape(HY_ORDER, 1, BRANCH_W)
    grid = (ncb, B)
    zspec = lambda part: pl.BlockSpec((L, cb), lambda j, b: (0, b * zb + part * ncb + j))
    wspec = lambda part: pl.BlockSpec((3, cb), lambda j, b: (0, part * ncb + j))
    bspec = lambda part: pl.BlockSpec((1, cb), lambda j, b: (0, part * ncb + j))
    aspec = pl.BlockSpec((L, cb), lambda j, b: (0, b * ncb + j))
    dspec = pl.BlockSpec((L, L), lambda j, b: (0, 0))
    ospec = lambda o: pl.BlockSpec((1, L, cb), lambda j, b: (o, 0, j))
    kspec = lambda o: pl.BlockSpec((1, 1, cb), lambda j, b: (o, 0, j))
    cp = _cparams(("parallel", "arbitrary"))
    act = lambda dt: jax.ShapeDtypeStruct((L, B * BRANCH_W), dt)

    zcur = None
    for o in range(HY_ORDER):
        first = o == 0
        if first:
            ins, args = [zspec(0), wspec(0), bspec(0)], [z2d, cw, cbias]
        else:
            ins, args = [aspec], [zcur]
        yr, yn = pl.pallas_call(
            functools.partial(_hy_fwd_kernel, first=first), grid=grid,
            in_specs=ins + [ospec(o), ospec(o), kspec(o), dspec, dspec],
            out_specs=[aspec, aspec], out_shape=[act(BF16), act(BF16)],
            compiler_params=cp, name=f"hyena_fwd{o}",
        )(*args, P, Q, KL, fc, fs)
        if first:
            zins, zargs = [zspec(0), wspec(0), bspec(0)], [z2d, cw, cbias]
        else:
            zins, zargs = [aspec], [zcur]
        zcur = pl.pallas_call(
            functools.partial(_hy_inv_kernel, first=first), grid=grid,
            in_specs=[aspec, aspec, dspec, dspec] + zins + [zspec(o + 1), wspec(o + 1), bspec(o + 1), kspec(o)],
            out_specs=aspec, out_shape=act(F32),
            compiler_params=cp, name=f"hyena_inv{o}",
        )(yr, yn, gc, gs, *zargs, z2d, cw, cbias, bias)
    return zcur


S5_BLK_STATES = (LANES // S5_GROUP_CH) * S5_STATE


def _s5_param_kernel(lr_ref, li_ref, ls_ref, br_ref, bi_ref, lbr_ref, lbi_ref, bbr_ref, bbi_ref):
    lr, li = lr_ref[...], li_ref[...]
    step = jnp.exp(ls_ref[...])
    mag = jnp.exp(lr * step)
    lbr = mag * jnp.cos(li * step)
    lbi = mag * jnp.sin(li * step)
    den = lr * lr + li * li
    nr = lbr - 1.0
    cr = (nr * lr + lbi * li) / den
    ci = (lbi * lr - nr * li) / den
    br, bi = br_ref[...], bi_ref[...]
    lbr_ref[...] = lbr
    lbi_ref[...] = lbi
    bbr_ref[...] = cr * br - ci * bi
    bbi_ref[...] = cr * bi + ci * br


def _s5_params(lp):
    n = 2 * S5_GROUPS * S5_STATE
    col = lambda a: a.reshape(n, 1)
    ls = jnp.broadcast_to(lp['s5_log_step'][:, :, None], (2, S5_GROUPS, S5_STATE))
    tr = 1024
    cspec = pl.BlockSpec((tr, 1), lambda i: (i, 0))
    bspec = pl.BlockSpec((tr, S5_GROUP_CH), lambda i: (i, 0))
    lbr, lbi, bbr, bbi = pl.pallas_call(
        _s5_param_kernel, grid=(n // tr,),
        in_specs=[cspec, cspec, cspec, bspec, bspec],
        out_specs=[cspec, cspec, bspec, bspec],
        out_shape=[jax.ShapeDtypeStruct((n, 1), F32)] * 2 + [jax.ShapeDtypeStruct((n, S5_GROUP_CH), F32)] * 2,
        compiler_params=_cparams(("parallel",)),
        name="s5_params",
    )(col(lp['s5_lambda_re']), col(lp['s5_lambda_im']), col(ls),
      lp['s5_B_re'].reshape(n, S5_GROUP_CH), lp['s5_B_im'].reshape(n, S5_GROUP_CH))
    gp = (2, S5_GROUPS, S5_STATE)
    return lbr.reshape(gp), lbi.reshape(gp), bbr.reshape(gp + (S5_GROUP_CH,)), bbi.reshape(gp + (S5_GROUP_CH,))


def _s5_block_weights(lbr, lbi, bbr, bbi, c_re, c_im):
    gl = LANES // S5_GROUP_CH
    nb = S5_GROUPS // gl
    eye = jnp.eye(gl, dtype=F32)

    def in_blk(bb):
        t = jnp.transpose(bb.reshape(2, nb, gl, S5_STATE, S5_GROUP_CH), (0, 1, 2, 4, 3))
        t = t[:, :, :, :, None, :] * eye[None, None, :, None, :, None]
        return t.reshape(2, nb, LANES, S5_BLK_STATES)

    def out_blk(c):
        t = jnp.transpose(c.reshape(2, nb, gl, S5_GROUP_CH, S5_STATE), (0, 1, 2, 4, 3))
        t = t[:, :, :, :, None, :] * eye[None, None, :, None, :, None]
        return t.reshape(2, nb, S5_BLK_STATES, LANES)

    w_in = jnp.concatenate([in_blk(bbr), in_blk(bbi)], axis=-1).astype(BF16)
    lam = jnp.stack([lbr.reshape(2, nb, S5_BLK_STATES), lbi.reshape(2, nb, S5_BLK_STATES)], axis=2)
    lam = jnp.transpose(lam, (1, 0, 2, 3)).reshape(nb, 4, S5_BLK_STATES)
    return w_in, lam, out_blk(c_re).astype(BF16), out_blk(c_im).astype(BF16)


def _s5_kernel(*refs, b):
    uf_refs, ub_refs = refs[:b], refs[b:2 * b]
    (wf_ref, wb_ref, cfr_ref, cfi_ref, cbr_ref, cbi_ref, lam_ref, s0_ref,
     yf_ref, yb_ref, fin_ref, xf_scr, xb_scr, st_scr, uf_il, ub_il) = refs[2 * b:]
    tt = uf_refs[0].shape[0]
    ns = S5_BLK_STATES
    i = pl.program_id(1)

    @pl.when(i == 0)
    def _():
        st_scr[...] = s0_ref[0]

    for n in range(b):
        uf_il[pl.ds(n, tt, stride=b), :] = uf_refs[n][...]
        ub_il[pl.ds(n, tt, stride=b), :] = ub_refs[n][...]
    xf_scr[...] = _dot(uf_il[...].astype(BF16), wf_ref[0, 0])
    xb_scr[...] = _dot(ub_il[...].astype(BF16), wb_ref[0, 0])
    lam = lam_ref[0]
    lfr, lfi, lbr, lbi = lam[0:1], lam[1:2], lam[2:3], lam[3:4]

    def step(t, carry):
        sfr, sfi, sbr, sbi = carry
        rf = pl.ds(pl.multiple_of(t * b, b), b)
        nfr = lfr * sfr - lfi * sfi + xf_scr[rf, :ns]
        nfi = lfr * sfi + lfi * sfr + xf_scr[rf, ns:]
        xf_scr[rf, :ns] = nfr
        xf_scr[rf, ns:] = nfi
        rb = pl.ds(pl.multiple_of((tt - 1 - t) * b, b), b)
        nbr = lbr * sbr - lbi * sbi + xb_scr[rb, :ns]
        nbi = lbr * sbi + lbi * sbr + xb_scr[rb, ns:]
        xb_scr[rb, :ns] = nbr
        xb_scr[rb, ns:] = nbi
        return nfr, nfi, nbr, nbi

    init = (st_scr[0, :, :ns], st_scr[0, :, ns:], st_scr[1, :, :ns], st_scr[1, :, ns:])
    sfr, sfi, sbr, sbi = lax.fori_loop(0, tt, step, init)
    st_scr[0, :, :ns] = sfr
    st_scr[0, :, ns:] = sfi
    st_scr[1, :, :ns] = sbr
    st_scr[1, :, ns:] = sbi

    yf = _dot(xf_scr[:, :ns].astype(BF16), cfr_ref[0, 0]) - _dot(xf_scr[:, ns:].astype(BF16), cfi_ref[0, 0])
    yb = _dot(xb_scr[:, :ns].astype(BF16), cbr_ref[0, 0]) - _dot(xb_scr[:, ns:].astype(BF16), cbi_ref[0, 0])
    yf_ref[...] = yf.reshape(tt, b, LANES)
    yb_ref[...] = yb.reshape(tt, b, LANES)

    @pl.when(i == pl.num_programs(1) - 1)
    def _():
        fin_ref[0] = st_scr[...]


def _s5(z2d, B, blk, s0, *, tt=64):
    L = z2d.shape[0]
    w_in, lam, c_re, c_im = blk
    nb = w_in.shape[1]
    tt = min(tt, L)
    nt = L // tt
    u0 = Z_S5_IN // LANES
    zb = NZ // LANES
    ns2 = 2 * S5_BLK_STATES
    wspec = lambda d: pl.BlockSpec((1, 1, LANES, ns2), lambda j, i: (d, j, 0, 0))
    cspec = lambda d: pl.BlockSpec((1, 1, S5_BLK_STATES, LANES), lambda j, i: (d, j, 0, 0))
    fwd = [pl.BlockSpec((tt, LANES), lambda j, i, n=n: (i, n * zb + u0 + j)) for n in range(B)]
    bwd = [pl.BlockSpec((tt, LANES), lambda j, i, n=n: (nt - 1 - i, n * zb + u0 + j)) for n in range(B)]
    return pl.pallas_call(
        functools.partial(_s5_kernel, b=B), grid=(nb, nt),
        in_specs=fwd + bwd + [
                  wspec(0), wspec(1), cspec(0), cspec(0), cspec(1), cspec(1),
                  pl.BlockSpec((1, 4, S5_BLK_STATES), lambda j, i: (j, 0, 0)),
                  pl.BlockSpec((1, 2, B, ns2), lambda j, i: (j, 0, 0, 0))],
        out_specs=[pl.BlockSpec((tt, B, LANES), lambda j, i: (i, 0, j)),
                   pl.BlockSpec((tt, B, LANES), lambda j, i: (nt - 1 - i, 0, j)),
                   pl.BlockSpec((1, 2, B, ns2), lambda j, i: (j, 0, 0, 0))],
        out_shape=[jax.ShapeDtypeStruct((L, B, BRANCH_W), F32)] * 2
                  + [jax.ShapeDtypeStruct((nb, 2, B, ns2), F32)],
        scratch_shapes=[pltpu.VMEM((tt * B, ns2), F32), pltpu.VMEM((tt * B, ns2), F32),
                        pltpu.VMEM((2, B, ns2), F32),
                        pltpu.VMEM((tt * B, LANES), F32), pltpu.VMEM((tt * B, LANES), F32)],
        compiler_params=_cparams(("parallel", "arbitrary")),
        name="s5_scan",
    )(*([z2d] * (2 * B)), w_in, w_in, c_re, c_im, c_re, c_im, lam, s0)


def _s5_state_to_blocks(s0):
    B = s0.shape[0]
    gl = LANES // S5_GROUP_CH
    nb = S5_GROUPS // gl
    t = s0.reshape(B, 2, nb, gl * S5_STATE, 2)
    t = jnp.transpose(t, (2, 1, 0, 4, 3))
    return t.reshape(nb, 2, B, 2 * S5_BLK_STATES)


def _s5_blocks_to_state(fin):
    nb, _, B, _ = fin.shape
    t = fin.reshape(nb, 2, B, 2, S5_BLK_STATES)
    t = jnp.transpose(t, (2, 1, 0, 4, 3))
    return t.reshape(B, 2, S5_GROUPS, S5_STATE, 2)


GATE_SLOTS = 8


def _gdn_gate_kernel(ab_ref, alog_ref, dtb_ref, col_ref, row_ref):
    L = ab_ref.shape[0]
    c = GDN_CHUNK
    slot = lax.broadcasted_iota(jnp.int32, (c, LANES), 1) % GATE_SLOTS
    r = lax.broadcasted_iota(jnp.int32, (c, c), 0)
    s = lax.broadcasted_iota(jnp.int32, (c, c), 1)
    lower = jnp.where(r >= s, 1.0, 0.0).astype(BF16)
    upper = jnp.where(r <= s, 1.0, 0.0).astype(BF16)
    neg_a = -jnp.exp(alog_ref[...])
    dtb = dtb_ref[...]

    def body(n, carry):
        rows = pl.ds(pl.multiple_of(n * c, c), c)
        x = ab_ref[rows, :]
        xa = x + dtb
        softplus = jnp.maximum(xa, 0.0) + jnp.log1p(jnp.exp(-jnp.abs(xa)))
        g = jnp.where((slot == 2) | (slot == 3), neg_a * softplus, 0.0)
        g1 = g.astype(BF16)
        r1 = g - g1.astype(F32)
        g2 = r1.astype(BF16)
        g3 = (r1 - g2.astype(F32)).astype(BF16)
        cf = _dot(lower, g1) + _dot(lower, g2) + _dot(lower, g3)
        cb = _dot(upper, g1) + _dot(upper, g2) + _dot(upper, g3)
        tile = jnp.where(slot < 2, jax.nn.sigmoid(x), jnp.where(slot == 2, cf, cb))
        col_ref[rows, :] = tile
        row_ref[0, pl.ds(n, 1)] = tile.T[None]
        return carry

    lax.fori_loop(0, L // c, body, 0)


def _gdn_gates(z2d, B, lp):
    L = z2d.shape[0]
    c = GDN_CHUNK
    ab0 = Z_AB // LANES
    zb = NZ // LANES

    def pad(a):
        t = jnp.zeros((GDN_HEADS, GATE_SLOTS), F32).at[:, 2:4].set(a.T)
        return jnp.pad(t.reshape(1, -1), ((0, 0), (0, LANES - GDN_HEADS * GATE_SLOTS)))

    return pl.pallas_call(
        _gdn_gate_kernel, grid=(B,),
        in_specs=[pl.BlockSpec((L, LANES), lambda b: (0, b * zb + ab0)),
                  pl.BlockSpec((1, LANES), lambda b: (0, 0)),
                  pl.BlockSpec((1, LANES), lambda b: (0, 0))],
        out_specs=[pl.BlockSpec((L, LANES), lambda b: (0, b)),
                   pl.BlockSpec((1, L // c, LANES, c), lambda b: (b, 0, 0, 0))],
        out_shape=[jax.ShapeDtypeStruct((L, B * LANES), F32),
                   jax.ShapeDtypeStruct((B, L // c, LANES, c), F32)],
        compiler_params=_cparams(("parallel",)),
        name="gdn_gates",
    )(z2d, pad(lp['gdn_A_log']), pad(lp['gdn_dt_bias']))


def _l2norm(x):
    return x * lax.rsqrt(jnp.sum(x * x, axis=-1, keepdims=True) + NORM_EPS)


def _chunk_masks():
    c = GDN_CHUNK
    r = lax.broadcasted_iota(jnp.int32, (c, c), 0)
    s = lax.broadcasted_iota(jnp.int32, (c, c), 1)
    return ((r >= s, r > s), (r <= s, r < s))


def _head_gates(col_ref, rows, h):
    tile = col_ref[rows, :]
    return pltpu.roll(tile, LANES - GATE_SLOTS * h, axis=1)[:, :GATE_SLOTS]


def _chunk_decay(gates, row_ref, ci, h, d, incl):
    beta = gates[:, d:d + 1]
    gcum = gates[:, 2 + d:3 + d]
    grow = row_ref[0, ci, pl.ds(GATE_SLOTS * h + 2 + d, 1), :]
    decay = jnp.where(incl, jnp.exp(jnp.where(incl, gcum - grow, 0.0)), 0.0)
    return beta, gcum, decay


def _dwconv3_rows(x_ref, w, r0, n):
    L = x_ref.shape[0]
    x = x_ref[pl.ds(r0, n), :]
    before = x_ref[pl.ds(jnp.maximum(r0 - 1, 0), 1), :]
    after = x_ref[pl.ds(jnp.minimum(r0 + n, L - 1), 1), :]
    before = jnp.where(r0 == 0, 0.0, before)
    after = jnp.where(r0 + n == L, 0.0, after)
    r = lax.broadcasted_iota(jnp.int32, x.shape, 0)
    prev = jnp.where(r == 0, before, pltpu.roll(x, 1, axis=0))
    nxt = jnp.where(r == n - 1, after, pltpu.roll(x, n - 1, axis=0))
    return w[0:1] * prev + w[1:2] * x + w[2:3] * nxt


def _gdn_amat_kernel(k_ref, wk_ref, col_ref, row_ref, a_ref):
    c = GDN_CHUNK
    nc = k_ref.shape[0] // c
    h = pl.program_id(1)
    wk = wk_ref[...]
    masks = _chunk_masks()

    group = 4 if nc % 4 == 0 else 1

    def body(g, carry):
        prepared = []
        for u in range(group):
            ci = g * group + u
            r0 = pl.multiple_of(ci * c, c)
            kc = _l2norm(_silu(_dwconv3_rows(k_ref, wk, r0, c)))
            prepared.append((ci, kc, _head_gates(col_ref, pl.ds(r0, c), h)))
        products = []
        for ci, kc, gates in prepared:
            kb16 = kc.astype(BF16)
            kk = lax.dot_general(kb16, kb16, (((1,), (1,)), ((), ())), preferred_element_type=F32)
            products.append((ci, kk, gates))
        for ci, kk, gates in products:
            for d in range(2):
                beta, _, decay = _chunk_decay(gates, row_ref, ci, h, d, masks[d][0])
                a_ref[d, 0, 0, pl.ds(ci, 1)] = jnp.where(masks[d][1], kk * (decay * beta), 0.0).astype(BF16)[None]
        return carry

    lax.fori_loop(0, nc // group, body, 0)


def _tri_inverse_kernel(a_ref, t_ref, a_scr, t_scr):
    n, _, pw = a_scr.shape
    a_scr[...] = a_ref[0].astype(F32)
    t_scr[...] = jnp.zeros(t_scr.shape, F32)

    def solve(upper):
        for i in (range(n - 1, -1, -1) if upper else range(n)):
            lo, hi = (i + 1, n) if upper else (0, i)
            c0, c1 = ((i // 8) * 8, n) if upper else (0, (i // 8 + 1) * 8)
            col = lax.broadcasted_iota(jnp.int32, (c1 - c0, pw), 0) + c0

            def inner(j, acc, i=i, c0=c0, c1=c1):
                return acc - a_scr[i, pl.ds(j, 1), :] * t_scr[j, c0:c1, :]

            t_scr[i, c0:c1, :] = lax.fori_loop(lo, hi, inner, jnp.where(col == i, 1.0, 0.0), unroll=2)

    @pl.when(pl.program_id(0) == 0)
    def _():
        solve(False)

    @pl.when(pl.program_id(0) == 1)
    def _():
        solve(True)

    t_ref[0] = t_scr[...].astype(BF16)


def _tri_inverse(a):
    _, c, _, P = a.shape
    pw = 2 * LANES if P % (2 * LANES) == 0 else LANES
    pp = -(-P // pw) * pw
    if pp != P:
        a = jnp.pad(a, ((0, 0), (0, 0), (0, 0), (0, pp - P)))
    spec = pl.BlockSpec((1, c, c, pw), lambda d, p: (d, 0, 0, p))
    t = pl.pallas_call(
        _tri_inverse_kernel, grid=(2, pp // pw),
        in_specs=[spec], out_specs=spec, out_shape=jax.ShapeDtypeStruct((2, c, c, pp), BF16),
        scratch_shapes=[pltpu.VMEM((c, c, pw), F32), pltpu.VMEM((c, c, pw), F32)],
        compiler_params=_cparams(("parallel", "parallel")),
        name="gdn_tri_inverse",
    )(a)
    return t[..., :P] if pp != P else t


def _gdn_kernel(q_ref, k_ref, v_ref, wq_ref, wk_ref, wv_ref, col_ref, row_ref, t_ref, ng_ref, s0_ref,
                o_ref, sfin_ref, mo_scr, n_scr, ou_scr, gl_scr, st_scr):
    L = q_ref.shape[0]
    c = GDN_CHUNK
    nc = L // c
    h = pl.program_id(1)
    wq, wk, wv = wq_ref[...], wk_ref[...], wv_ref[...]
    st_scr[...] = s0_ref[0, :, 0]
    masks = _chunk_masks()

    def g_last_of(gcum, d):
        return gcum[c - 1:c] if d == 0 else gcum[0:1]

    group = 2 if nc % 2 == 0 else 1

    def phase1(g, carry):
        work = []
        for u in range(group):
            ci = g * group + u
            r0 = pl.multiple_of(ci * c, c)
            qc = _l2norm(_silu(_dwconv3_rows(q_ref, wq, r0, c))) * (GDN_DK ** -0.5)
            kc = _l2norm(_silu(_dwconv3_rows(k_ref, wk, r0, c)))
            vc = _silu(_dwconv3_rows(v_ref, wv, r0, c))
            gates = _head_gates(col_ref, pl.ds(r0, c), h)
            kb16 = kc.astype(BF16)
            qk_raw = lax.dot_general(qc.astype(BF16), kb16, (((1,), (1,)), ((), ())), preferred_element_type=F32)
            k_t = kc.T
            for d in range(2):
                _, gcum, decay = _chunk_decay(gates, row_ref, ci, h, d, masks[d][0])
                beta_row = row_ref[0, ci, pl.ds(GATE_SLOTS * h + d, 1), :]
                gc_row = row_ref[0, ci, pl.ds(GATE_SLOTS * h + 2 + d, 1), :]
                t_mat = t_ref[d, 0, 0, pl.ds(ci, 1)][0].astype(F32)
                u = _bdot(t_mat * beta_row, vc)
                w = _bdot(t_mat * (beta_row * jnp.exp(gc_row)), kb16)
                g_last = g_last_of(gcum, d)
                kdec_t = k_t * jnp.exp(g_last - gc_row)
                gl_scr[d, pl.ds(ci, 1), :] = jnp.broadcast_to(jnp.exp(g_last), (1, GDN_DV))
                work.append((ci, r0, d, jnp.concatenate([u, w], axis=1), kdec_t, qk_raw * decay,
                             qc * jnp.exp(gcum)))
        results = []
        for ci, r0, d, uw, kdec_t, qk, qe in work:
            lhs = jnp.concatenate([kdec_t, qk], axis=0)
            results.append((ci, r0, d, qe, _bdot(lhs, uw)))
        for ci, r0, d, qe, res in results:
            n_scr[d, pl.ds(ci, 1)] = res[:GDN_DK, :GDN_DV][None]
            mo = jnp.concatenate([-res[:GDN_DK, GDN_DV:], qe - res[GDN_DK:, GDN_DV:]], axis=0)
            mo_scr[d, pl.ds(ci, 1)] = mo.astype(BF16)[None]
            ou_scr[d, pl.ds(r0, c), :] = res[GDN_DK:, :GDN_DV]
        return carry

    lax.fori_loop(0, nc // group, phase1, 0)

    def advance(ci, d):
        rows = pl.ds(pl.multiple_of(ci * c, c), c)
        state = st_scr[d]
        r = _dot(mo_scr[d, pl.ds(ci, 1)][0], state.astype(BF16))
        st_scr[d] = state * gl_scr[d, pl.ds(ci, 1), :] + r[:GDN_DK] + n_scr[d, pl.ds(ci, 1)][0]
        return rows, r[GDN_DK:]

    def phase2(n, carry):
        rows, o = advance(n, 0)
        o_ref[rows, :] = o + ou_scr[0, rows, :]
        rows, o = advance(nc - 1 - n, 1)
        ou_scr[1, rows, :] = o + ou_scr[1, rows, :]
        return carry

    lax.fori_loop(0, nc, phase2, 0)
    o = o_ref[...] + ou_scr[1]
    o_ref[...] = o * lax.rsqrt(jnp.mean(o * o, axis=-1, keepdims=True) + NORM_EPS) * ng_ref[...]
    sfin_ref[0, :, 0] = st_scr[...]


def _gdn(z2d, B, lp, s0):
    L = z2d.shape[0]
    H = GDN_HEADS
    c = GDN_CHUNK
    nc = L // c
    cols, rows = _gdn_gates(z2d, B, lp)
    zb = NZ // LANES
    q0 = Z_GDN_IN // LANES
    zspec = lambda part: pl.BlockSpec((L, LANES), lambda b, h: (0, b * zb + q0 + part * H + h))
    wspec = lambda part: pl.BlockSpec((3, LANES), lambda b, h: (0, part * H + h))
    colspec = pl.BlockSpec((L, LANES), lambda b, h: (0, b))
    rowspec = pl.BlockSpec((1, nc, LANES, c), lambda b, h: (b, 0, 0, 0))
    tspec = pl.BlockSpec((2, 1, 1, nc, c, c), lambda b, h: (0, b, h, 0, 0, 0))
    sspec = pl.BlockSpec((1, 2, 1, GDN_DK, GDN_DV), lambda b, h: (b, 0, h, 0, 0))
    cw = lp['gdn_conv_w']
    cp = _cparams(("parallel", "parallel"))
    a_mat = pl.pallas_call(
        _gdn_amat_kernel, grid=(B, H),
        in_specs=[zspec(1), wspec(1), colspec, rowspec],
        out_specs=tspec, out_shape=jax.ShapeDtypeStruct((2, B, H, nc, c, c), BF16),
        compiler_params=cp, name="gdn_amat",
    )(z2d, cw, cols, rows)
    P = B * H * nc
    a_t = jnp.swapaxes(a_mat.reshape(2, P, c * c), 1, 2).reshape(2, c, c, P)
    t_t = _tri_inverse(a_t)
    t_mat = jnp.swapaxes(t_t.reshape(2, c * c, P), 1, 2).reshape(2, B, H, nc, c, c)
    return pl.pallas_call(
        _gdn_kernel, grid=(B, H),
        in_specs=[zspec(0), zspec(1), zspec(2), wspec(0), wspec(1), wspec(2), colspec, rowspec, tspec,
                  pl.BlockSpec((1, LANES), lambda b, h: (0, 0)), sspec],
        out_specs=[pl.BlockSpec((L, LANES), lambda b, h: (0, b * H + h)), sspec],
        out_shape=[jax.ShapeDtypeStruct((L, B * BRANCH_W), F32),
                   jax.ShapeDtypeStruct((B, 2, H, GDN_DK, GDN_DV), F32)],
        scratch_shapes=[pltpu.VMEM((2, nc, GDN_DK + c, GDN_DV), BF16),
                          pltpu.VMEM((2, nc, GDN_DK, GDN_DV), F32),
                          pltpu.VMEM((2, L, GDN_DV), F32),
                          pltpu.VMEM((2, nc, GDN_DV), F32),
                          pltpu.VMEM((2, GDN_DK, GDN_DV), F32)],
        compiler_params=cp, name="gdn",
    )(z2d, z2d, z2d, cw, cw, cw, cols, rows, t_mat, lp['gdn_norm_g'].reshape(1, LANES), s0)


def _merge_kernel(x_ref, hy_ref, yf_ref, yb_ref, og_ref, hyg_ref, u_ref, s5g_ref, gdg_ref,
                  m0_ref, m1_ref, m2_ref, d_ref, gw_ref, gb_ref, wb_ref, wo_ref, gate_ref, fg_ref,
                  out_ref, *, final_norm):
    b = pl.program_id(0)
    y_hy = hy_ref[...] * _silu(hyg_ref[...])
    s5_raw = d_ref[...] * u_ref[...] + yf_ref[...] + yb_ref[...]
    gelu = 0.5 * s5_raw * (1.0 + jnp.tanh(math.sqrt(2.0 / math.pi) * (s5_raw + 0.044715 * (s5_raw * s5_raw * s5_raw))))
    glu = _dot(gelu.astype(BF16), gw_ref[...]) + gb_ref[...]
    y_s5 = glu[:, :BRANCH_W] * jax.nn.sigmoid(glu[:, BRANCH_W:]) * _silu(s5g_ref[...])
    y_gdn = og_ref[...] * _silu(gdg_ref[...])
    merged = jax.nn.sigmoid(m0_ref[...]) * _dot(y_hy.astype(BF16), wb_ref[0])
    merged += jax.nn.sigmoid(m1_ref[...]) * _dot(y_s5.astype(BF16), wb_ref[1])
    merged += jax.nn.sigmoid(m2_ref[...]) * _dot(y_gdn.astype(BF16), wb_ref[2])
    x = x_ref[...] + gate_ref[pl.ds(b, 1), :] * _dot(merged.astype(BF16), wo_ref[...])
    if final_norm:
        out_ref[0] = x * lax.rsqrt(jnp.mean(x * x, axis=-1, keepdims=True) + NORM_EPS) * fg_ref[...]
    else:
        out_ref[...] = x


def _merge(x, hy, yf, yb, og, z2d, gate, lp, final_g, *, rows_per_tile=256):
    L = x.shape[0]
    B = gate.shape[0]
    tl = min(L, rows_per_tile)
    final = final_g is not None
    zb = NZ // BRANCH_W
    act = pl.BlockSpec((tl, BRANCH_W), lambda b, i: (i, b))
    zspec = lambda col: pl.BlockSpec((tl, BRANCH_W), lambda b, i: (i, b * zb + col // BRANCH_W))
    once = pl.Buffered(1)
    const = lambda *shape: pl.BlockSpec(shape, lambda b, i: (0,) * len(shape), pipeline_mode=once)
    return pl.pallas_call(
        functools.partial(_merge_kernel, final_norm=final),
        grid=(B, L // tl),
        in_specs=[act, act, act, act, act,
                  zspec(Z_HY_GATE), zspec(Z_S5_IN), zspec(Z_S5_GATE), zspec(Z_GDN_GATE),
                  zspec(Z_MERGE), zspec(Z_MERGE + D_MODEL), zspec(Z_MERGE + 2 * D_MODEL),
                  const(1, BRANCH_W), const(BRANCH_W, 2 * BRANCH_W), const(1, 2 * BRANCH_W),
                  const(3, BRANCH_W, D_MODEL), const(D_MODEL, D_MODEL),
                  const(B, D_MODEL), const(1, D_MODEL)],
        out_specs=pl.BlockSpec((1, tl, D_MODEL), lambda b, i: (b, i, 0)) if final else act,
        out_shape=jax.ShapeDtypeStruct((B, L, D_MODEL) if final else (L, B * D_MODEL), F32),
        compiler_params=_cparams(("parallel", "parallel")),
        name="merge",
    )(x, hy, yf, yb, og, z2d, z2d, z2d, z2d, z2d, z2d, z2d,
      lp['s5_D'].reshape(1, BRANCH_W), lp['s5_glu_w'].astype(BF16), lp['s5_glu_b'].reshape(1, -1),
      lp['w_branch'].astype(BF16), lp['w_out'].astype(BF16), gate,
      (final_g if final_g is not None else jnp.ones((D_MODEL,), F32)).reshape(1, D_MODEL))


def _pack_w_in(w_in):
    ab = w_in[:, W_IN_AB:W_IN_AB + 4 * GDN_HEADS].reshape(D_MODEL, 4, GDN_HEADS)
    ab = jnp.pad(jnp.transpose(ab, (0, 2, 1)), ((0, 0), (0, 0), (0, GATE_SLOTS - 4)))
    ab = ab.reshape(D_MODEL, GDN_HEADS * GATE_SLOTS)
    pad = jnp.zeros((D_MODEL, NZ - Z_AB - GDN_HEADS * GATE_SLOTS), w_in.dtype)
    return jnp.concatenate([w_in[:, :W_IN_AB], w_in[:, W_IN_AB + 4 * GDN_HEADS:], ab, pad], axis=1).astype(BF16)


def _trunk_layer(x, pos, mod, lp, shared, s0_gdn, s0_s5, final_g, *, entry):
    B = mod.shape[0]
    mod3 = jnp.transpose(mod.reshape(B, 3, D_MODEL), (1, 0, 2))
    z2d, x = _in_projection(x, pos, mod3, lp['norm_g'], shared['w_packed'], entry=entry)
    L = z2d.shape[0]
    tables = shared['tables'][L]
    spectra = _hyena_filter_spectra(L, lp, (tables['zpos'], tables['decay'], tables['dft'][0], tables['dft'][1]))
    hy = _hyena(z2d, B, lp, spectra, tables['dft'])
    yf, yb, s5_fin = _s5(z2d, B, shared['s5_blk'], _s5_state_to_blocks(s0_s5))
    og, gdn_fin = _gdn(z2d, B, lp, s0_gdn)
    x = _merge(x, hy, yf.reshape(L, B * BRANCH_W), yb.reshape(L, B * BRANCH_W), og, z2d, mod3[2], lp, final_g)
    return x, gdn_fin, _s5_blocks_to_state(s5_fin)


def _grid_pos_embed(n_tokens, dim):
    rows = n_tokens // GRID_W
    t = jnp.arange(rows * GRID_W)
    r = (t // GRID_W).astype(F32)
    col = (t % GRID_W).astype(F32)
    quarter = dim // 4
    omega = 1.0 / (10000.0 ** (jnp.arange(quarter, dtype=F32) / quarter))
    er = r[:, None] * omega
    ec = col[:, None] * omega
    return jnp.concatenate([jnp.sin(er), jnp.cos(er), jnp.sin(ec), jnp.cos(ec)], axis=-1)


_LAYER_KEYS = ('norm_g', 'w_in', 'hy_conv_w', 'hy_conv_b', 'hy_f_w1', 'hy_f_b1', 'hy_f_w2', 'hy_f_b2',
               'hy_f_w3', 'hy_f_freq', 'hy_bias', 's5_lambda_re', 's5_lambda_im', 's5_log_step',
               's5_B_re', 's5_B_im', 's5_C_re', 's5_C_im', 's5_D', 's5_glu_w', 's5_glu_b', 'gdn_conv_w',
               'gdn_A_log', 'gdn_dt_bias', 'gdn_norm_g', 'w_branch', 'w_out')


def kernel(x_prompt, x_sample, c, c_ctx, state_gdn, state_s5, w_mod, b_mod, norm_g, w_in, hy_conv_w, hy_conv_b, hy_f_w1, hy_f_b1, hy_f_w2, hy_f_b2, hy_f_w3, hy_f_freq, hy_bias, s5_lambda_re, s5_lambda_im, s5_log_step, s5_B_re, s5_B_im, s5_C_re, s5_C_im, s5_D, s5_glu_w, s5_glu_b, gdn_conv_w, gdn_A_log, gdn_dt_bias, gdn_norm_g, w_branch, w_out, final_norm_g):
    params = dict(norm_g=norm_g, w_in=w_in, hy_conv_w=hy_conv_w, hy_conv_b=hy_conv_b, hy_f_w1=hy_f_w1,
                  hy_f_b1=hy_f_b1, hy_f_w2=hy_f_w2, hy_f_b2=hy_f_b2, hy_f_w3=hy_f_w3, hy_f_freq=hy_f_freq,
                  hy_bias=hy_bias, s5_lambda_re=s5_lambda_re, s5_lambda_im=s5_lambda_im,
                  s5_log_step=s5_log_step, s5_B_re=s5_B_re, s5_B_im=s5_B_im, s5_C_re=s5_C_re,
                  s5_C_im=s5_C_im, s5_D=s5_D, s5_glu_w=s5_glu_w, s5_glu_b=s5_glu_b, gdn_conv_w=gdn_conv_w,
                  gdn_A_log=gdn_A_log, gdn_dt_bias=gdn_dt_bias, gdn_norm_g=gdn_norm_g, w_branch=w_branch,
                  w_out=w_out)
    depth = w_in.shape[0]
    bp, lc, _ = x_prompt.shape
    bl, ll, _ = x_sample.shape

    cc = jnp.concatenate([c_ctx[None], c], axis=0)
    n_rows = -(-cc.shape[0] // 8) * 8
    cc = jnp.pad(cc, ((0, n_rows - cc.shape[0]), (0, 0)))
    mods = _modulation(cc, w_mod, b_mod)

    tables = {}
    for L in {lc, ll}:
        zpos, decay = _filter_tables(L)
        tables[L] = dict(zpos=zpos, decay=decay, dft=_dft_tables(L))

    xc, xl = x_prompt, x_sample
    pos = _grid_pos_embed(ll, D_MODEL).astype(x_sample.dtype)
    zero_gdn = jnp.zeros((bp, 2, GDN_HEADS, GDN_DK, GDN_DV), F32)
    zero_s5 = jnp.zeros((bp, 2, S5_GROUPS, S5_STATE, 2), F32)
    gdn_states, s5_states = [], []
    for l in range(depth):
        lp = {k: params[k][l] for k in _LAYER_KEYS}
        lbr, lbi, bbr, bbi = _s5_params(lp)
        shared = dict(w_packed=_pack_w_in(lp['w_in']), tables=tables,
                      s5_blk=_s5_block_weights(lbr, lbi, bbr, bbi, lp['s5_C_re'], lp['s5_C_im']))
        fg = final_norm_g if l == depth - 1 else None
        mod_ctx = jnp.broadcast_to(mods[l, 0:1], (bp, 3 * D_MODEL))
        mod_lat = mods[l, 1:1 + bl]
        xc, fin_gdn, fin_s5 = _trunk_layer(xc, None, mod_ctx, lp, shared, zero_gdn, zero_s5, fg, entry=l == 0)
        gdn_states.append(fin_gdn)
        s5_states.append(fin_s5)
        xl, _, _ = _trunk_layer(xl, pos if l == 0 else None, mod_lat, lp, shared,
                                state_gdn[:, l].astype(F32), state_s5[:, l].astype(F32), fg, entry=l == 0)
    return (xc, xl, jnp.stack(gdn_states, axis=1), jnp.stack(s5_states, axis=1))
```

```python
import functools
import math

import jax
import jax.numpy as jnp
from jax import lax
from jax.experimental import pallas as pl
from jax.experimental.pallas import tpu as pltpu

F32 = jnp.float32
BF16 = jnp.bfloat16

D_MODEL = 1024
BRANCH_W = 1024
GRID_W = 64
HY_ORDER = 2
HY_EMB = 33
HY_HID = 64
HY_DECAY_TARGET = 1e-2
HY_SHORT_DECAY_PCT = 0.3
HY_LONG_DECAY_PCT = 1.5
S5_GROUP_CH = 16
S5_GROUPS = 64
S5_STATE = 64
GDN_DK = 128
GDN_DV = 128
GDN_HEADS = 8
GDN_CHUNK = 64
NORM_EPS = 1e-6

LANES = 128
Z_HY_IN, Z_HY_GATE, Z_S5_IN, Z_S5_GATE = 0, 3072, 4096, 5120
Z_GDN_IN, Z_GDN_GATE, Z_MERGE, Z_AB = 6144, 9216, 10240, 13312
NZ = 14336
W_IN_AB = 9216
VMEM_LIMIT_V7X = 56 * 1024 * 1024


def _cparams(sem):
    return pltpu.CompilerParams(dimension_semantics=sem, vmem_limit_bytes=VMEM_LIMIT_V7X)


def _dot(a, b):
    return jnp.dot(a, b, preferred_element_type=F32)


def _bdot(a, b):
    return jnp.dot(a.astype(BF16), b.astype(BF16), preferred_element_type=F32)


def _split(a):
    hi = a.astype(BF16)
    lo = (a - hi.astype(F32)).astype(BF16)
    return hi, lo


def _dot3(a, b):
    ah, al = _split(a)
    bh, bl = _split(b)
    return _dot(ah, bh) + _dot(al, bh) + _dot(ah, bl)


def _silu(x):
    return x * jax.nn.sigmoid(x)


def _shift_prev(x):
    r = lax.broadcasted_iota(jnp.int32, x.shape, 0)
    return jnp.where(r == 0, 0.0, pltpu.roll(x, 1, axis=0))


def _shift_next(x):
    n = x.shape[0]
    r = lax.broadcasted_iota(jnp.int32, x.shape, 0)
    return jnp.where(r == n - 1, 0.0, pltpu.roll(x, n - 1, axis=0))


def _dwconv3(x, w):
    return w[0:1] * _shift_prev(x) + w[1:2] * x + w[2:3] * _shift_next(x)


def _mod_kernel(c_ref, w_ref, b_ref, o_ref):
    o_ref[0] = _dot3(_silu(c_ref[...]), w_ref[0]) + b_ref[0]


def _modulation(cc, w_mod, b_mod):
    depth = w_mod.shape[0]
    r = cc.shape[0]
    return pl.pallas_call(
        _mod_kernel,
        grid=(depth, 3),
        in_specs=[pl.BlockSpec((r, D_MODEL), lambda l, n: (0, 0)),
                  pl.BlockSpec((1, D_MODEL, D_MODEL), lambda l, n: (l, 0, n)),
                  pl.BlockSpec((1, 1, D_MODEL), lambda l, n: (l, 0, n))],
        out_specs=pl.BlockSpec((1, r, D_MODEL), lambda l, n: (l, 0, n)),
        out_shape=jax.ShapeDtypeStruct((depth, r, 3 * D_MODEL), F32),
        compiler_params=_cparams(("arbitrary", "arbitrary")),
        name="modulation",
    )(cc, w_mod, b_mod.reshape(depth, 1, 3 * D_MODEL))


def _inproj_kernel(*refs, entry, has_pos):
    refs = list(refs)
    x_ref = refs.pop(0)
    pos_ref = refs.pop(0) if has_pos else None
    mod_ref, g_ref, w_ref, z_ref = refs[:4]
    xres_ref = refs[4] if entry else None
    h_scr = refs[-1]
    b = pl.program_id(0)

    @pl.when(pl.program_id(2) == 0)
    def _():
        x = x_ref[0] if entry else x_ref[...]
        if has_pos:
            x = x + pos_ref[...]
        if entry:
            xres_ref[...] = x
        y = x * lax.rsqrt(jnp.mean(x * x, axis=-1, keepdims=True) + NORM_EPS) * g_ref[...]
        h_scr[...] = (y * (1.0 + mod_ref[1, pl.ds(b, 1), :]) + mod_ref[0, pl.ds(b, 1), :]).astype(BF16)

    z_ref[...] = _dot(h_scr[...], w_ref[...])


def _in_projection(x, pos, mod3, norm_g, w_packed, *, entry, rows_per_tile=1024, tn=2048):
    B = mod3.shape[1]
    L = x.shape[1] if entry else x.shape[0]
    has_pos = pos is not None
    tl = min(L, rows_per_tile)
    if entry and tl == rows_per_tile:
        tn //= 2
    nj = NZ // tn
    if entry:
        in_specs = [pl.BlockSpec((1, tl, D_MODEL), lambda b, i, j: (b, i, 0))]
    else:
        in_specs = [pl.BlockSpec((tl, D_MODEL), lambda b, i, j: (i, b))]
    args = [x]
    if has_pos:
        in_specs.append(pl.BlockSpec((tl, D_MODEL), lambda b, i, j: (i, 0)))
        args.append(pos)
    in_specs += [pl.BlockSpec((3, B, D_MODEL), lambda b, i, j: (0, 0, 0)),
                 pl.BlockSpec((1, D_MODEL), lambda b, i, j: (0, 0)),
                 pl.BlockSpec((D_MODEL, tn), lambda b, i, j: (0, j))]
    args += [mod3, norm_g.reshape(1, D_MODEL), w_packed]
    out_specs = [pl.BlockSpec((tl, tn), lambda b, i, j: (i, b * nj + j))]
    out_shape = [jax.ShapeDtypeStruct((L, B * NZ), F32)]
    if entry:
        out_specs.append(pl.BlockSpec((tl, D_MODEL), lambda b, i, j: (i, b)))
        out_shape.append(jax.ShapeDtypeStruct((L, B * D_MODEL), F32))
    res = pl.pallas_call(
        functools.partial(_inproj_kernel, entry=entry, has_pos=has_pos),
        grid=(B, L // tl, nj), in_specs=in_specs, out_specs=out_specs, out_shape=out_shape,
        scratch_shapes=[pltpu.VMEM((tl, D_MODEL), BF16)],
        compiler_params=_cparams(("parallel", "parallel", "arbitrary")),
        name="in_projection",
    )(*args)
    return (res[0], res[1]) if entry else (res[0], x)


def _dft_tables(L):
    f = lax.broadcasted_iota(jnp.int32, (L, L), 0)
    t = lax.broadcasted_iota(jnp.int32, (L, L), 1)
    ang = ((f * t) % (2 * L)).astype(F32) * (math.pi / L)
    c = jnp.cos(ang).astype(BF16)
    s = jnp.sin(ang)
    fwd_s = jnp.where(f == 0, jnp.where(t % 2 == 0, 1.0, -1.0), s).astype(BF16)
    inv_s = jnp.where(t == 0, jnp.where(f % 2 == 0, 1.0, -1.0), s).astype(BF16)
    return c, fwd_s, c, inv_s


def _filter_tables(L):
    t = jnp.linspace(0.0, 1.0, L, dtype=F32)[:, None]
    bands = (HY_EMB - 1) // 2
    fb = jnp.linspace(1e-4, bands - 1, bands, dtype=F32)[None, :]
    wpos = 2.0 * math.pi * jnp.arange(L, dtype=F32)[:, None] / L
    zpos = jnp.concatenate([t, jnp.cos(fb * wpos), -jnp.sin(fb * wpos)], axis=-1)
    deltas = jnp.abs(jnp.linspace(math.log(HY_DECAY_TARGET) / HY_LONG_DECAY_PCT,
                                  math.log(HY_DECAY_TARGET) / HY_SHORT_DECAY_PCT, BRANCH_W, dtype=F32))
    decay = jnp.exp(-t * deltas)
    return zpos, decay


def _hy_filter_kernel(zpos_ref, w1_ref, b1_ref, w2_ref, b2_ref, freq_ref, w3_ref, decay_ref,
                      fc_ref, fs_ref, p_ref, q_ref, kl_ref):
    freq = freq_ref[...]
    hdn = jnp.sin(freq * (_dot3(zpos_ref[...], w1_ref[...]) + b1_ref[...]))
    hdn = jnp.sin(freq * (_dot3(hdn, w2_ref[...]) + b2_ref[...]))
    decay = decay_ref[...]
    L = decay.shape[0]
    row = lax.broadcasted_iota(jnp.int32, decay.shape, 0)
    alt = jnp.where(row % 2 == 0, 1.0, -1.0)
    bin_w = jnp.where(row == 0, 1.0, 2.0) / (2.0 * L)
    for o in range(HY_ORDER):
        fwd = _dot3(hdn, w3_ref[2 * o]) * decay
        bwd = jnp.where(row == 0, 0.0, _dot3(hdn, w3_ref[2 * o + 1]) * decay)
        norm = jnp.sum(jnp.abs(fwd) + jnp.abs(bwd), axis=0, keepdims=True)
        a = (fwd + bwd) / norm
        s = (fwd - bwd) / norm
        p_ref[o] = _bdot(fc_ref[...], a) * bin_w
        q_ref[o] = jnp.where(row == 0, 0.0, _bdot(fs_ref[...], s) * bin_w)
        kl_ref[o] = jnp.sum(alt * a, axis=0, keepdims=True) / (2.0 * L)


def _hyena_filter_spectra(L, lp, tables, *, cb=256):
    zpos, decay, fc, fs = tables
    w3 = jnp.transpose(lp['hy_f_w3'].reshape(HY_HID, HY_ORDER * 2, BRANCH_W), (1, 0, 2))
    const = lambda *shape: pl.BlockSpec(shape, lambda j: (0,) * len(shape))
    return pl.pallas_call(
        _hy_filter_kernel,
        grid=(BRANCH_W // cb,),
        in_specs=[const(L, HY_EMB), const(HY_EMB, HY_HID), const(1, HY_HID), const(HY_HID, HY_HID),
                  const(1, HY_HID), const(1, HY_HID),
                  pl.BlockSpec((HY_ORDER * 2, HY_HID, cb), lambda j: (0, 0, j)),
                  pl.BlockSpec((L, cb), lambda j: (0, j)),
                  const(L, L), const(L, L)],
        out_specs=[pl.BlockSpec((HY_ORDER, L, cb), lambda j: (0, 0, j)),
                   pl.BlockSpec((HY_ORDER, L, cb), lambda j: (0, 0, j)),
                   pl.BlockSpec((HY_ORDER, 1, cb), lambda j: (0, 0, j))],
        out_shape=[jax.ShapeDtypeStruct((HY_ORDER, L, BRANCH_W), F32),
                   jax.ShapeDtypeStruct((HY_ORDER, L, BRANCH_W), F32),
                   jax.ShapeDtypeStruct((HY_ORDER, 1, BRANCH_W), F32)],
        compiler_params=_cparams(("arbitrary",)),
        name="hyena_filter",
    )(zpos, lp['hy_f_w1'], lp['hy_f_b1'].reshape(1, HY_HID), lp['hy_f_w2'], lp['hy_f_b2'].reshape(1, HY_HID),
      lp['hy_f_freq'].reshape(1, HY_HID), w3, decay, fc, fs)


def _hy_fwd_kernel(*refs, first):
    if first:
        zin_ref, cw_ref, cbias_ref, p_ref, q_ref, kl_ref, fc_ref, fs_ref, yr_ref, yn_ref = refs
        zin = _dwconv3(zin_ref[...], cw_ref[...]) + cbias_ref[...]
    else:
        zin_ref, p_ref, q_ref, kl_ref, fc_ref, fs_ref, yr_ref, yn_ref = refs
        zin = zin_ref[...]
    zb = zin.astype(BF16)
    zr = _dot(fc_ref[...], zb)
    w = _dot(fs_ref[...], zb)
    p = p_ref[0]
    q = q_ref[0]
    row = lax.broadcasted_iota(jnp.int32, p.shape, 0)
    pd = jnp.where(row == 0, kl_ref[0], p)
    yr_ref[...] = (zr * p - w * q).astype(BF16)
    yn_ref[...] = (zr * q + w * pd).astype(BF16)


def _hy_inv_kernel(*refs, first):
    if first:
        (yr_ref, yn_ref, gc_ref, gs_ref, zin_ref, cwz_ref, cbz_ref, xg_ref, cwx_ref, cbx_ref,
         bias_ref, out_ref) = refs
        zin = _dwconv3(zin_ref[...], cwz_ref[...]) + cbz_ref[...]
    else:
        yr_ref, yn_ref, gc_ref, gs_ref, zin_ref, xg_ref, cwx_ref, cbx_ref, bias_ref, out_ref = refs
        zin = zin_ref[...]
    conv = _dot(gc_ref[...], yr_ref[...]) + _dot(gs_ref[...], yn_ref[...])
    xg = _dwconv3(xg_ref[...], cwx_ref[...]) + cbx_ref[...]
    out_ref[...] = xg * (conv + bias_ref[0] * zin)


def _hyena(z2d, B, lp, spectra, dft, *, cb=256):
    L = z2d.shape[0]
    P, Q, KL = spectra
    fc, fs, gc, gs = dft
    ncb = BRANCH_W // cb
    zb = NZ // cb
    cw = lp['hy_conv_w']
    cbias = lp['hy_conv_b'].reshape(1, -1)
    bias = lp['hy_bias'].reshape(HY_ORDER, 1, BRANCH_W)
    grid = (ncb, B)
    zspec = lambda part: pl.BlockSpec((L, cb), lambda j, b: (0, b * zb + part * ncb + j))
    wspec = lambda part: pl.BlockSpec((3, cb), lambda j, b: (0, part * ncb + j))
    bspec = lambda part: pl.BlockSpec((1, cb), lambda j, b: (0, part * ncb + j))
    aspec = pl.BlockSpec((L, cb), lambda j, b: (0, b * ncb + j))
    dspec = pl.BlockSpec((L, L), lambda j, b: (0, 0))
    ospec = lambda o: pl.BlockSpec((1, L, cb), lambda j, b: (o, 0, j))
    kspec = lambda o: pl.BlockSpec((1, 1, cb), lambda j, b: (o, 0, j))
    cp = _cparams(("parallel", "arbitrary"))
    act = lambda dt: jax.ShapeDtypeStruct((L, B * BRANCH_W), dt)

    zcur = None
    for o in range(HY_ORDER):
        first = o == 0
        if first:
            ins, args = [zspec(0), wspec(0), bspec(0)], [z2d, cw, cbias]
        else:
            ins, args = [aspec], [zcur]
        yr, yn = pl.pallas_call(
            functools.partial(_hy_fwd_kernel, first=first), grid=grid,
            in_specs=ins + [ospec(o), ospec(o), kspec(o), dspec, dspec],
            out_specs=[aspec, aspec], out_shape=[act(BF16), act(BF16)],
            compiler_params=cp, name=f"hyena_fwd{o}",
        )(*args, P, Q, KL, fc, fs)
        if first:
            zins, zargs = [zspec(0), wspec(0), bspec(0)], [z2d, cw, cbias]
        else:
            zins, zargs = [aspec], [zcur]
        zcur = pl.pallas_call(
            functools.partial(_hy_inv_kernel, first=first), grid=grid,
            in_specs=[aspec, aspec, dspec, dspec] + zins + [zspec(o + 1), wspec(o + 1), bspec(o + 1), kspec(o)],
            out_specs=aspec, out_shape=act(F32),
            compiler_params=cp, name=f"hyena_inv{o}",
        )(yr, yn, gc, gs, *zargs, z2d, cw, cbias, bias)
    return zcur


S5_BLK_STATES = (LANES // S5_GROUP_CH) * S5_STATE


def _s5_param_kernel(lr_ref, li_ref, ls_ref, br_ref, bi_ref, lbr_ref, lbi_ref, bbr_ref, bbi_ref):
    lr, li = lr_ref[...], li_ref[...]
    step = jnp.exp(ls_ref[...])
    mag = jnp.exp(lr * step)
    lbr = mag * jnp.cos(li * step)
    lbi = mag * jnp.sin(li * step)
    den = lr * lr + li * li
    nr = lbr - 1.0
    cr = (nr * lr + lbi * li) / den
    ci = (lbi * lr - nr * li) / den
    br, bi = br_ref[...], bi_ref[...]
    lbr_ref[...] = lbr
    lbi_ref[...] = lbi
    bbr_ref[...] = cr * br - ci * bi
    bbi_ref[...] = cr * bi + ci * br


def _s5_params(lp):
    n = 2 * S5_GROUPS * S5_STATE
    col = lambda a: a.reshape(n, 1)
    ls = jnp.broadcast_to(lp['s5_log_step'][:, :, None], (2, S5_GROUPS, S5_STATE))
    tr = 1024
    cspec = pl.BlockSpec((tr, 1), lambda i: (i, 0))
    bspec = pl.BlockSpec((tr, S5_GROUP_CH), lambda i: (i, 0))
    lbr, lbi, bbr, bbi = pl.pallas_call(
        _s5_param_kernel, grid=(n // tr,),
        in_specs=[cspec, cspec, cspec, bspec, bspec],
        out_specs=[cspec, cspec, bspec, bspec],
        out_shape=[jax.ShapeDtypeStruct((n, 1), F32)] * 2 + [jax.ShapeDtypeStruct((n, S5_GROUP_CH), F32)] * 2,
        compiler_params=_cparams(("parallel",)),
        name="s5_params",
    )(col(lp['s5_lambda_re']), col(lp['s5_lambda_im']), col(ls),
      lp['s5_B_re'].reshape(n, S5_GROUP_CH), lp['s5_B_im'].reshape(n, S5_GROUP_CH))
    gp = (2, S5_GROUPS, S5_STATE)
    return lbr.reshape(gp), lbi.reshape(gp), bbr.reshape(gp + (S5_GROUP_CH,)), bbi.reshape(gp + (S5_GROUP_CH,))


def _s5_block_weights(lbr, lbi, bbr, bbi, c_re, c_im):
    gl = LANES // S5_GROUP_CH
    nb = S5_GROUPS // gl
    eye = jnp.eye(gl, dtype=F32)

    def in_blk(bb):
        t = jnp.transpose(bb.reshape(2, nb, gl, S5_STATE, S5_GROUP_CH), (0, 1, 2, 4, 3))
        t = t[:, :, :, :, None, :] * eye[None, None, :, None, :, None]
        return t.reshape(2, nb, LANES, S5_BLK_STATES)

    def out_blk(c):
        t = jnp.transpose(c.reshape(2, nb, gl, S5_GROUP_CH, S5_STATE), (0, 1, 2, 4, 3))
        t = t[:, :, :, :, None, :] * eye[None, None, :, None, :, None]
        return t.reshape(2, nb, S5_BLK_STATES, LANES)

    w_in = jnp.concatenate([in_blk(bbr), in_blk(bbi)], axis=-1).astype(BF16)
    lam = jnp.stack([lbr.reshape(2, nb, S5_BLK_STATES), lbi.reshape(2, nb, S5_BLK_STATES)], axis=2)
    lam = jnp.transpose(lam, (1, 0, 2, 3)).reshape(nb, 4, S5_BLK_STATES)
    return w_in, lam, out_blk(c_re).astype(BF16), out_blk(c_im).astype(BF16)


def _s5_kernel(*refs, b):
    uf_refs, ub_refs = refs[:b], refs[b:2 * b]
    (wf_ref, wb_ref, cfr_ref, cfi_ref, cbr_ref, cbi_ref, lam_ref, s0_ref,
     yf_ref, yb_ref, fin_ref, xf_scr, xb_scr, st_scr, uf_il, ub_il) = refs[2 * b:]
    tt = uf_refs[0].shape[0]
    ns = S5_BLK_STATES
    i = pl.program_id(1)

    @pl.when(i == 0)
    def _():
        st_scr[...] = s0_ref[0]

    for n in range(b):
        uf_il[pl.ds(n, tt, stride=b), :] = uf_refs[n][...]
        ub_il[pl.ds(n, tt, stride=b), :] = ub_refs[n][...]
    xf_scr[...] = _dot(uf_il[...].astype(BF16), wf_ref[0, 0])
    xb_scr[...] = _dot(ub_il[...].astype(BF16), wb_ref[0, 0])
    lam = lam_ref[0]
    lfr, lfi, lbr, lbi = lam[0:1], lam[1:2], lam[2:3], lam[3:4]

    def step(t, carry):
        sfr, sfi, sbr, sbi = carry
        rf = pl.ds(pl.multiple_of(t * b, b), b)
        nfr = lfr * sfr - lfi * sfi + xf_scr[rf, :ns]
        nfi = lfr * sfi + lfi * sfr + xf_scr[rf, ns:]
        xf_scr[rf, :ns] = nfr
        xf_scr[rf, ns:] = nfi
        rb = pl.ds(pl.multiple_of((tt - 1 - t) * b, b), b)
        nbr = lbr * sbr - lbi * sbi + xb_scr[rb, :ns]
        nbi = lbr * sbi + lbi * sbr + xb_scr[rb, ns:]
        xb_scr[rb, :ns] = nbr
        xb_scr[rb, ns:] = nbi
        return nfr, nfi, nbr, nbi

    init = (st_scr[0, :, :ns], st_scr[0, :, ns:], st_scr[1, :, :ns], st_scr[1, :, ns:])
    sfr, sfi, sbr, sbi = lax.fori_loop(0, tt, step, init)
    st_scr[0, :, :ns] = sfr
    st_scr[0, :, ns:] = sfi
    st_scr[1, :, :ns] = sbr
    st_scr[1, :, ns:] = sbi

    yf = _dot(xf_scr[:, :ns].astype(BF16), cfr_ref[0, 0]) - _dot(xf_scr[:, ns:].astype(BF16), cfi_ref[0, 0])
    yb = _dot(xb_scr[:, :ns].astype(BF16), cbr_ref[0, 0]) - _dot(xb_scr[:, ns:].astype(BF16), cbi_ref[0, 0])
    yf_ref[...] = yf.reshape(tt, b, LANES)
    yb_ref[...] = yb.reshape(tt, b, LANES)

    @pl.when(i == pl.num_programs(1) - 1)
    def _():
        fin_ref[0] = st_scr[...]


def _s5(z2d, B, blk, s0, *, tt=64):
    L = z2d.shape[0]
    w_in, lam, c_re, c_im = blk
    nb = w_in.shape[1]
    tt = min(tt, L)
    nt = L // tt
    u0 = Z_S5_IN // LANES
    zb = NZ // LANES
    ns2 = 2 * S5_BLK_STATES
    wspec = lambda d: pl.BlockSpec((1, 1, LANES, ns2), lambda j, i: (d, j, 0, 0))
    cspec = lambda d: pl.BlockSpec((1, 1, S5_BLK_STATES, LANES), lambda j, i: (d, j, 0, 0))
    fwd = [pl.BlockSpec((tt, LANES), lambda j, i, n=n: (i, n * zb + u0 + j)) for n in range(B)]
    bwd = [pl.BlockSpec((tt, LANES), lambda j, i, n=n: (nt - 1 - i, n * zb + u0 + j)) for n in range(B)]
    return pl.pallas_call(
        functools.partial(_s5_kernel, b=B), grid=(nb, nt),
        in_specs=fwd + bwd + [
                  wspec(0), wspec(1), cspec(0), cspec(0), cspec(1), cspec(1),
                  pl.BlockSpec((1, 4, S5_BLK_STATES), lambda j, i: (j, 0, 0)),
                  pl.BlockSpec((1, 2, B, ns2), lambda j, i: (j, 0, 0, 0))],
        out_specs=[pl.BlockSpec((tt, B, LANES), lambda j, i: (i, 0, j)),
                   pl.BlockSpec((tt, B, LANES), lambda j, i: (nt - 1 - i, 0, j)),
                   pl.BlockSpec((1, 2, B, ns2), lambda j, i: (j, 0, 0, 0))],
        out_shape=[jax.ShapeDtypeStruct((L, B, BRANCH_W), F32)] * 2
                  + [jax.ShapeDtypeStruct((nb, 2, B, ns2), F32)],
        scratch_shapes=[pltpu.VMEM((tt * B, ns2), F32), pltpu.VMEM((tt * B, ns2), F32),
                        pltpu.VMEM((2, B, ns2), F32),
                        pltpu.VMEM((tt * B, LANES), F32), pltpu.VMEM((tt * B, LANES), F32)],
        compiler_params=_cparams(("parallel", "arbitrary")),
        name="s5_scan",
    )(*([z2d] * (2 * B)), w_in, w_in, c_re, c_im, c_re, c_im, lam, s0)


def _s5_state_to_blocks(s0):
    B = s0.shape[0]
    gl = LANES // S5_GROUP_CH
    nb = S5_GROUPS // gl
    t = s0.reshape(B, 2, nb, gl * S5_STATE, 2)
    t = jnp.transpose(t, (2, 1, 0, 4, 3))
    return t.reshape(nb, 2, B, 2 * S5_BLK_STATES)


def _s5_blocks_to_state(fin):
    nb, _, B, _ = fin.shape
    t = fin.reshape(nb, 2, B, 2, S5_BLK_STATES)
    t = jnp.transpose(t, (2, 1, 0, 4, 3))
    return t.reshape(B, 2, S5_GROUPS, S5_STATE, 2)


GATE_SLOTS = 8


def _gdn_gate_kernel(ab_ref, alog_ref, dtb_ref, col_ref, row_ref):
    L = ab_ref.shape[0]
    c = GDN_CHUNK
    slot = lax.broadcasted_iota(jnp.int32, (c, LANES), 1) % GATE_SLOTS
    r = lax.broadcasted_iota(jnp.int32, (c, c), 0)
    s = lax.broadcasted_iota(jnp.int32, (c, c), 1)
    lower = jnp.where(r >= s, 1.0, 0.0).astype(BF16)
    upper = jnp.where(r <= s, 1.0, 0.0).astype(BF16)
    neg_a = -jnp.exp(alog_ref[...])
    dtb = dtb_ref[...]

    def body(n, carry):
        rows = pl.ds(pl.multiple_of(n * c, c), c)
        x = ab_ref[rows, :]
        xa = x + dtb
        softplus = jnp.maximum(xa, 0.0) + jnp.log1p(jnp.exp(-jnp.abs(xa)))
        g = jnp.where((slot == 2) | (slot == 3), neg_a * softplus, 0.0)
        g1 = g.astype(BF16)
        r1 = g - g1.astype(F32)
        g2 = r1.astype(BF16)
        g3 = (r1 - g2.astype(F32)).astype(BF16)
        cf = _dot(lower, g1) + _dot(lower, g2) + _dot(lower, g3)
        cb = _dot(upper, g1) + _dot(upper, g2) + _dot(upper, g3)
        tile = jnp.where(slot < 2, jax.nn.sigmoid(x), jnp.where(slot == 2, cf, cb))
        col_ref[rows, :] = tile
        row_ref[0, pl.ds(n, 1)] = tile.T[None]
        return carry

    lax.fori_loop(0, L // c, body, 0)


def _gdn_gates(z2d, B, lp):
    L = z2d.shape[0]
    c = GDN_CHUNK
    ab0 = Z_AB // LANES
    zb = NZ // LANES

    def pad(a):
        t = jnp.zeros((GDN_HEADS, GATE_SLOTS), F32).at[:, 2:4].set(a.T)
        return jnp.pad(t.reshape(1, -1), ((0, 0), (0, LANES - GDN_HEADS * GATE_SLOTS)))

    return pl.pallas_call(
        _gdn_gate_kernel, grid=(B,),
        in_specs=[pl.BlockSpec((L, LANES), lambda b: (0, b * zb + ab0)),
                  pl.BlockSpec((1, LANES), lambda b: (0, 0)),
                  pl.BlockSpec((1, LANES), lambda b: (0, 0))],
        out_specs=[pl.BlockSpec((L, LANES), lambda b: (0, b)),
                   pl.BlockSpec((1, L // c, LANES, c), lambda b: (b, 0, 0, 0))],
        out_shape=[jax.ShapeDtypeStruct((L, B * LANES), F32),
                   jax.ShapeDtypeStruct((B, L // c, LANES, c), F32)],
        compiler_params=_cparams(("parallel",)),
        name="gdn_gates",
    )(z2d, pad(lp['gdn_A_log']), pad(lp['gdn_dt_bias']))


def _l2norm(x):
    return x * lax.rsqrt(jnp.sum(x * x, axis=-1, keepdims=True) + NORM_EPS)


def _chunk_masks():
    c = GDN_CHUNK
    r = lax.broadcasted_iota(jnp.int32, (c, c), 0)
    s = lax.broadcasted_iota(jnp.int32, (c, c), 1)
    return ((r >= s, r > s), (r <= s, r < s))


def _head_gates(col_ref, rows, h):
    tile = col_ref[rows, :]
    return pltpu.roll(tile, LANES - GATE_SLOTS * h, axis=1)[:, :GATE_SLOTS]


def _chunk_decay(gates, row_ref, ci, h, d, incl):
    beta = gates[:, d:d + 1]
    gcum = gates[:, 2 + d:3 + d]
    grow = row_ref[0, ci, pl.ds(GATE_SLOTS * h + 2 + d, 1), :]
    decay = jnp.where(incl, jnp.exp(jnp.where(incl, gcum - grow, 0.0)), 0.0)
    return beta, gcum, decay


def _dwconv3_rows(x_ref, w, r0, n):
    L = x_ref.shape[0]
    x = x_ref[pl.ds(r0, n), :]
    before = x_ref[pl.ds(jnp.maximum(r0 - 1, 0), 1), :]
    after = x_ref[pl.ds(jnp.minimum(r0 + n, L - 1), 1), :]
    before = jnp.where(r0 == 0, 0.0, before)
    after = jnp.where(r0 + n == L, 0.0, after)
    r = lax.broadcasted_iota(jnp.int32, x.shape, 0)
    prev = jnp.where(r == 0, before, pltpu.roll(x, 1, axis=0))
    nxt = jnp.where(r == n - 1, after, pltpu.roll(x, n - 1, axis=0))
    return w[0:1] * prev + w[1:2] * x + w[2:3] * nxt


def _gdn_amat_kernel(k_ref, wk_ref, col_ref, row_ref, a_ref):
    c = GDN_CHUNK
    nc = k_ref.shape[0] // c
    h = pl.program_id(1)
    wk = wk_ref[...]
    masks = _chunk_masks()

    group = 4 if nc % 4 == 0 else 1

    def body(g, carry):
        prepared = []
        for u in range(group):
            ci = g * group + u
            r0 = pl.multiple_of(ci * c, c)
            kc = _l2norm(_silu(_dwconv3_rows(k_ref, wk, r0, c)))
            prepared.append((ci, kc, _head_gates(col_ref, pl.ds(r0, c), h)))
        products = []
        for ci, kc, gates in prepared:
            kb16 = kc.astype(BF16)
            kk = lax.dot_general(kb16, kb16, (((1,), (1,)), ((), ())), preferred_element_type=F32)
            products.append((ci, kk, gates))
        for ci, kk, gates in products:
            for d in range(2):
                beta, _, decay = _chunk_decay(gates, row_ref, ci, h, d, masks[d][0])
                a_ref[d, 0, 0, pl.ds(ci, 1)] = jnp.where(masks[d][1], kk * (decay * beta), 0.0).astype(BF16)[None]
        return carry

    lax.fori_loop(0, nc // group, body, 0)


def _tri_inverse_kernel(a_ref, t_ref, a_scr, t_scr):
    n, _, pw = a_scr.shape
    a_scr[...] = a_ref[0].astype(F32)
    t_scr[...] = jnp.zeros(t_scr.shape, F32)

    def solve(upper):
        for i in (range(n - 1, -1, -1) if upper else range(n)):
            lo, hi = (i + 1, n) if upper else (0, i)
            c0, c1 = ((i // 8) * 8, n) if upper else (0, (i // 8 + 1) * 8)
            col = lax.broadcasted_iota(jnp.int32, (c1 - c0, pw), 0) + c0

            def inner(j, acc, i=i, c0=c0, c1=c1):
                return acc - a_scr[i, pl.ds(j, 1), :] * t_scr[j, c0:c1, :]

            t_scr[i, c0:c1, :] = lax.fori_loop(lo, hi, inner, jnp.where(col == i, 1.0, 0.0), unroll=2)

    @pl.when(pl.program_id(0) == 0)
    def _():
        solve(False)

    @pl.when(pl.program_id(0) == 1)
    def _():
        solve(True)

    t_ref[0] = t_scr[...].astype(BF16)


def _tri_inverse(a):
    _, c, _, P = a.shape
    pw = 2 * LANES if P % (2 * LANES) == 0 else LANES
    pp = -(-P // pw) * pw
    if pp != P:
        a = jnp.pad(a, ((0, 0), (0, 0), (0, 0), (0, pp - P)))
    spec = pl.BlockSpec((1, c, c, pw), lambda d, p: (d, 0, 0, p))
    t = pl.pallas_call(
        _tri_inverse_kernel, grid=(2, pp // pw),
        in_specs=[spec], out_specs=spec, out_shape=jax.ShapeDtypeStruct((2, c, c, pp), BF16),
        scratch_shapes=[pltpu.VMEM((c, c, pw), F32), pltpu.VMEM((c, c, pw), F32)],
        compiler_params=_cparams(("parallel", "parallel")),
        name="gdn_tri_inverse",
    )(a)
    return t[..., :P] if pp != P else t


def _gdn_kernel(q_ref, k_ref, v_ref, wq_ref, wk_ref, wv_ref, col_ref, row_ref, t_ref, ng_ref, s0_ref,
                o_ref, sfin_ref, mo_scr, n_scr, ou_scr, gl_scr, st_scr):
    L = q_ref.shape[0]
    c = GDN_CHUNK
    nc = L // c
    h = pl.program_id(1)
    wq, wk, wv = wq_ref[...], wk_ref[...], wv_ref[...]
    st_scr[...] = s0_ref[0, :, 0]
    masks = _chunk_masks()

    def g_last_of(gcum, d):
        return gcum[c - 1:c] if d == 0 else gcum[0:1]

    group = 4 if nc % 4 == 0 else (2 if nc % 2 == 0 else 1)

    def phase1(g, carry):
        work = []
        for u in range(group):
            ci = g * group + u
            r0 = pl.multiple_of(ci * c, c)
            qc = _l2norm(_silu(_dwconv3_rows(q_ref, wq, r0, c))) * (GDN_DK ** -0.5)
            kc = _l2norm(_silu(_dwconv3_rows(k_ref, wk, r0, c)))
            vc = _silu(_dwconv3_rows(v_ref, wv, r0, c))
            gates = _head_gates(col_ref, pl.ds(r0, c), h)
            kb16 = kc.astype(BF16)
            qk_raw = lax.dot_general(qc.astype(BF16), kb16, (((1,), (1,)), ((), ())), preferred_element_type=F32)
            k_t = kc.T
            for d in range(2):
                _, gcum, decay = _chunk_decay(gates, row_ref, ci, h, d, masks[d][0])
                beta_row = row_ref[0, ci, pl.ds(GATE_SLOTS * h + d, 1), :]
                gc_row = row_ref[0, ci, pl.ds(GATE_SLOTS * h + 2 + d, 1), :]
                t_mat = t_ref[d, 0, 0, pl.ds(ci, 1)][0].astype(F32)
                u = _bdot(t_mat * beta_row, vc)
                w = _bdot(t_mat * (beta_row * jnp.exp(gc_row)), kb16)
                g_last = g_last_of(gcum, d)
                kdec_t = k_t * jnp.exp(g_last - gc_row)
                gl_scr[d, pl.ds(ci, 1), :] = jnp.broadcast_to(jnp.exp(g_last), (1, GDN_DV))
                work.append((ci, r0, d, jnp.concatenate([u, w], axis=1), kdec_t, qk_raw * decay,
                             qc * jnp.exp(gcum)))
        results = []
        for ci, r0, d, uw, kdec_t, qk, qe in work:
            lhs = jnp.concatenate([kdec_t, qk], axis=0)
            results.append((ci, r0, d, qe, _bdot(lhs, uw)))
        for ci, r0, d, qe, res in results:
            n_scr[d, pl.ds(ci, 1)] = res[:GDN_DK, :GDN_DV][None]
            mo = jnp.concatenate([-res[:GDN_DK, GDN_DV:], qe - res[GDN_DK:, GDN_DV:]], axis=0)
            mo_scr[d, pl.ds(ci, 1)] = mo.astype(BF16)[None]
            ou_scr[d, pl.ds(r0, c), :] = res[GDN_DK:, :GDN_DV]
        return carry

    lax.fori_loop(0, nc // group, phase1, 0)

    def advance(ci, d):
        rows = pl.ds(pl.multiple_of(ci * c, c), c)
        state = st_scr[d]
        r = _dot(mo_scr[d, pl.ds(ci, 1)][0], state.astype(BF16))
        st_scr[d] = state * gl_scr[d, pl.ds(ci, 1), :] + r[:GDN_DK] + n_scr[d, pl.ds(ci, 1)][0]
        return rows, r[GDN_DK:]

    def phase2(n, carry):
        rows, o = advance(n, 0)
        o_ref[rows, :] = o + ou_scr[0, rows, :]
        rows, o = advance(nc - 1 - n, 1)
        ou_scr[1, rows, :] = o + ou_scr[1, rows, :]
        return carry

    lax.fori_loop(0, nc, phase2, 0)
    o = o_ref[...] + ou_scr[1]
    o_ref[...] = o * lax.rsqrt(jnp.mean(o * o, axis=-1, keepdims=True) + NORM_EPS) * ng_ref[...]
    sfin_ref[0, :, 0] = st_scr[...]


def _gdn(z2d, B, lp, s0):
    L = z2d.shape[0]
    H = GDN_HEADS
    c = GDN_CHUNK
    nc = L // c
    cols, rows = _gdn_gates(z2d, B, lp)
    zb = NZ // LANES
    q0 = Z_GDN_IN // LANES
    zspec = lambda part: pl.BlockSpec((L, LANES), lambda b, h: (0, b * zb + q0 + part * H + h))
    wspec = lambda part: pl.BlockSpec((3, LANES), lambda b, h: (0, part * H + h))
    colspec = pl.BlockSpec((L, LANES), lambda b, h: (0, b))
    rowspec = pl.BlockSpec((1, nc, LANES, c), lambda b, h: (b, 0, 0, 0))
    tspec = pl.BlockSpec((2, 1, 1, nc, c, c), lambda b, h: (0, b, h, 0, 0, 0))
    sspec = pl.BlockSpec((1, 2, 1, GDN_DK, GDN_DV), lambda b, h: (b, 0, h, 0, 0))
    cw = lp['gdn_conv_w']
    cp = _cparams(("parallel", "parallel"))
    a_mat = pl.pallas_call(
        _gdn_amat_kernel, grid=(B, H),
        in_specs=[zspec(1), wspec(1), colspec, rowspec],
        out_specs=tspec, out_shape=jax.ShapeDtypeStruct((2, B, H, nc, c, c), BF16),
        compiler_params=cp, name="gdn_amat",
    )(z2d, cw, cols, rows)
    P = B * H * nc
    a_t = jnp.swapaxes(a_mat.reshape(2, P, c * c), 1, 2).reshape(2, c, c, P)
    t_t = _tri_inverse(a_t)
    t_mat = jnp.swapaxes(t_t.reshape(2, c * c, P), 1, 2).reshape(2, B, H, nc, c, c)
    return pl.pallas_call(
        _gdn_kernel, grid=(B, H),
        in_specs=[zspec(0), zspec(1), zspec(2), wspec(0), wspec(1), wspec(2), colspec, rowspec, tspec,
                  pl.BlockSpec((1, LANES), lambda b, h: (0, 0)), sspec],
        out_specs=[pl.BlockSpec((L, LANES), lambda b, h: (0, b * H + h)), sspec],
        out_shape=[jax.ShapeDtypeStruct((L, B * BRANCH_W), F32),
                   jax.ShapeDtypeStruct((B, 2, H, GDN_DK, GDN_DV), F32)],
        scratch_shapes=[pltpu.VMEM((2, nc, GDN_DK + c, GDN_DV), BF16),
                          pltpu.VMEM((2, nc, GDN_DK, GDN_DV), F32),
                          pltpu.VMEM((2, L, GDN_DV), F32),
                          pltpu.VMEM((2, nc, GDN_DV), F32),
                          pltpu.VMEM((2, GDN_DK, GDN_DV), F32)],
        compiler_params=cp, name="gdn",
    )(z2d, z2d, z2d, cw, cw, cw, cols, rows, t_mat, lp['gdn_norm_g'].reshape(1, LANES), s0)


def _merge_kernel(x_ref, hy_ref, yf_ref, yb_ref, og_ref, hyg_ref, u_ref, s5g_ref, gdg_ref,
                  m0_ref, m1_ref, m2_ref, d_ref, gw_ref, gb_ref, wb_ref, wo_ref, gate_ref, fg_ref,
                  out_ref, *, final_norm):
    b = pl.program_id(0)
    y_hy = hy_ref[...] * _silu(hyg_ref[...])
    s5_raw = d_ref[...] * u_ref[...] + yf_ref[...] + yb_ref[...]
    gelu = 0.5 * s5_raw * (1.0 + jnp.tanh(math.sqrt(2.0 / math.pi) * (s5_raw + 0.044715 * (s5_raw * s5_raw * s5_raw))))
    glu = _dot(gelu.astype(BF16), gw_ref[...]) + gb_ref[...]
    y_s5 = glu[:, :BRANCH_W] * jax.nn.sigmoid(glu[:, BRANCH_W:]) * _silu(s5g_ref[...])
    y_gdn = og_ref[...] * _silu(gdg_ref[...])
    merged = jax.nn.sigmoid(m0_ref[...]) * _dot(y_hy.astype(BF16), wb_ref[0])
    merged += jax.nn.sigmoid(m1_ref[...]) * _dot(y_s5.astype(BF16), wb_ref[1])
    merged += jax.nn.sigmoid(m2_ref[...]) * _dot(y_gdn.astype(BF16), wb_ref[2])
    x = x_ref[...] + gate_ref[pl.ds(b, 1), :] * _dot(merged.astype(BF16), wo_ref[...])
    if final_norm:
        out_ref[0] = x * lax.rsqrt(jnp.mean(x * x, axis=-1, keepdims=True) + NORM_EPS) * fg_ref[...]
    else:
        out_ref[...] = x


def _merge(x, hy, yf, yb, og, z2d, gate, lp, final_g, *, rows_per_tile=256):
    L = x.shape[0]
    B = gate.shape[0]
    tl = min(L, rows_per_tile)
    final = final_g is not None
    zb = NZ // BRANCH_W
    act = pl.BlockSpec((tl, BRANCH_W), lambda b, i: (i, b))
    zspec = lambda col: pl.BlockSpec((tl, BRANCH_W), lambda b, i: (i, b * zb + col // BRANCH_W))
    once = pl.Buffered(1)
    const = lambda *shape: pl.BlockSpec(shape, lambda b, i: (0,) * len(shape), pipeline_mode=once)
    return pl.pallas_call(
        functools.partial(_merge_kernel, final_norm=final),
        grid=(B, L // tl),
        in_specs=[act, act, act, act, act,
                  zspec(Z_HY_GATE), zspec(Z_S5_IN), zspec(Z_S5_GATE), zspec(Z_GDN_GATE),
                  zspec(Z_MERGE), zspec(Z_MERGE + D_MODEL), zspec(Z_MERGE + 2 * D_MODEL),
                  const(1, BRANCH_W), const(BRANCH_W, 2 * BRANCH_W), const(1, 2 * BRANCH_W),
                  const(3, BRANCH_W, D_MODEL), const(D_MODEL, D_MODEL),
                  const(B, D_MODEL), const(1, D_MODEL)],
        out_specs=pl.BlockSpec((1, tl, D_MODEL), lambda b, i: (b, i, 0)) if final else act,
        out_shape=jax.ShapeDtypeStruct((B, L, D_MODEL) if final else (L, B * D_MODEL), F32),
        compiler_params=_cparams(("parallel", "parallel")),
        name="merge",
    )(x, hy, yf, yb, og, z2d, z2d, z2d, z2d, z2d, z2d, z2d,
      lp['s5_D'].reshape(1, BRANCH_W), lp['s5_glu_w'].astype(BF16), lp['s5_glu_b'].reshape(1, -1),
      lp['w_branch'].astype(BF16), lp['w_out'].astype(BF16), gate,
      (final_g if final_g is not None else jnp.ones((D_MODEL,), F32)).reshape(1, D_MODEL))


def _pack_w_in(w_in):
    ab = w_in[:, W_IN_AB:W_IN_AB + 4 * GDN_HEADS].reshape(D_MODEL, 4, GDN_HEADS)
    ab = jnp.pad(jnp.transpose(ab, (0, 2, 1)), ((0, 0), (0, 0), (0, GATE_SLOTS - 4)))
    ab = ab.reshape(D_MODEL, GDN_HEADS * GATE_SLOTS)
    pad = jnp.zeros((D_MODEL, NZ - Z_AB - GDN_HEADS * GATE_SLOTS), w_in.dtype)
    return jnp.concatenate([w_in[:, :W_IN_AB], w_in[:, W_IN_AB + 4 * GDN_HEADS:], ab, pad], axis=1).astype(BF16)


def _trunk_layer(x, pos, mod, lp, shared, s0_gdn, s0_s5, final_g, *, entry):
    B = mod.shape[0]
    mod3 = jnp.transpose(mod.reshape(B, 3, D_MODEL), (1, 0, 2))
    z2d, x = _in_projection(x, pos, mod3, lp['norm_g'], shared['w_packed'], entry=entry)
    L = z2d.shape[0]
    tables = shared['tables'][L]
    spectra = _hyena_filter_spectra(L, lp, (tables['zpos'], tables['decay'], tables['dft'][0], tables['dft'][1]))
    hy = _hyena(z2d, B, lp, spectra, tables['dft'])
    yf, yb, s5_fin = _s5(z2d, B, shared['s5_blk'], _s5_state_to_blocks(s0_s5))
    og, gdn_fin = _gdn(z2d, B, lp, s0_gdn)
    x = _merge(x, hy, yf.reshape(L, B * BRANCH_W), yb.reshape(L, B * BRANCH_W), og, z2d, mod3[2], lp, final_g)
    return x, gdn_fin, _s5_blocks_to_state(s5_fin)


def _grid_pos_embed(n_tokens, dim):
    rows = n_tokens // GRID_W
    t = jnp.arange(rows * GRID_W)
    r = (t // GRID_W).astype(F32)
    col = (t % GRID_W).astype(F32)
    quarter = dim // 4
    omega = 1.0 / (10000.0 ** (jnp.arange(quarter, dtype=F32) / quarter))
    er = r[:, None] * omega
    ec = col[:, None] * omega
    return jnp.concatenate([jnp.sin(er), jnp.cos(er), jnp.sin(ec), jnp.cos(ec)], axis=-1)


_LAYER_KEYS = ('norm_g', 'w_in', 'hy_conv_w', 'hy_conv_b', 'hy_f_w1', 'hy_f_b1', 'hy_f_w2', 'hy_f_b2',
               'hy_f_w3', 'hy_f_freq', 'hy_bias', 's5_lambda_re', 's5_lambda_im', 's5_log_step',
               's5_B_re', 's5_B_im', 's5_C_re', 's5_C_im', 's5_D', 's5_glu_w', 's5_glu_b', 'gdn_conv_w',
               'gdn_A_log', 'gdn_dt_bias', 'gdn_norm_g', 'w_branch', 'w_out')


def kernel(x_prompt, x_sample, c, c_ctx, state_gdn, state_s5, w_mod, b_mod, norm_g, w_in, hy_conv_w, hy_conv_b, hy_f_w1, hy_f_b1, hy_f_w2, hy_f_b2, hy_f_w3, hy_f_freq, hy_bias, s5_lambda_re, s5_lambda_im, s5_log_step, s5_B_re, s5_B_im, s5_C_re, s5_C_im, s5_D, s5_glu_w, s5_glu_b, gdn_conv_w, gdn_A_log, gdn_dt_bias, gdn_norm_g, w_branch, w_out, final_norm_g):
    params = dict(norm_g=norm_g, w_in=w_in, hy_conv_w=hy_conv_w, hy_conv_b=hy_conv_b, hy_f_w1=hy_f_w1,
                  hy_f_b1=hy_f_b1, hy_f_w2=hy_f_w2, hy_f_b2=hy_f_b2, hy_f_w3=hy_f_w3, hy_f_freq=hy_f_freq,
                  hy_bias=hy_bias, s5_lambda_re=s5_lambda_re, s5_lambda_im=s5_lambda_im,
                  s5_log_step=s5_log_step, s5_B_re=s5_B_re, s5_B_im=s5_B_im, s5_C_re=s5_C_re,
                  s5_C_im=s5_C_im, s5_D=s5_D, s5_glu_w=s5_glu_w, s5_glu_b=s5_glu_b, gdn_conv_w=gdn_conv_w,
                  gdn_A_log=gdn_A_log, gdn_dt_bias=gdn_dt_bias, gdn_norm_g=gdn_norm_g, w_branch=w_branch,
                  w_out=w_out)
    depth = w_in.shape[0]
    bp, lc, _ = x_prompt.shape
    bl, ll, _ = x_sample.shape

    cc = jnp.concatenate([c_ctx[None], c], axis=0)
    n_rows = -(-cc.shape[0] // 8) * 8
    cc = jnp.pad(cc, ((0, n_rows - cc.shape[0]), (0, 0)))
    mods = _modulation(cc, w_mod, b_mod)

    tables = {}
    for L in {lc, ll}:
        zpos, decay = _filter_tables(L)
        tables[L] = dict(zpos=zpos, decay=decay, dft=_dft_tables(L))

    xc, xl = x_prompt, x_sample
    pos = _grid_pos_embed(ll, D_MODEL).astype(x_sample.dtype)
    zero_gdn = jnp.zeros((bp, 2, GDN_HEADS, GDN_DK, GDN_DV), F32)
    zero_s5 = jnp.zeros((bp, 2, S5_GROUPS, S5_STATE, 2), F32)
    gdn_states, s5_states = [], []
    for l in range(depth):
        lp = {k: params[k][l] for k in _LAYER_KEYS}
        lbr, lbi, bbr, bbi = _s5_params(lp)
        shared = dict(w_packed=_pack_w_in(lp['w_in']), tables=tables,
                      s5_blk=_s5_block_weights(lbr, lbi, bbr, bbi, lp['s5_C_re'], lp['s5_C_im']))
        fg = final_norm_g if l == depth - 1 else None
        mod_ctx = jnp.broadcast_to(mods[l, 0:1], (bp, 3 * D_MODEL))
        mod_lat = mods[l, 1:1 + bl]
        xc, fin_gdn, fin_s5 = _trunk_layer(xc, None, mod_ctx, lp, shared, zero_gdn, zero_s5, fg, entry=l == 0)
        gdn_states.append(fin_gdn)
        s5_states.append(fin_s5)
        xl, _, _ = _trunk_layer(xl, pos if l == 0 else None, mod_lat, lp, shared,
                                state_gdn[:, l].astype(F32), state_s5[:, l].astype(F32), fg, entry=l == 0)
    return (xc, xl, jnp.stack(gdn_states, axis=1), jnp.stack(s5_states, axis=1))
```
